```python
import jax
import jax.numpy as jnp
from jax import lax
import numpy as np

D_MODEL = 1024
BATCH = 2
SEQ = 8192
DEPTH = 2
DEC_BATCH = 32
DEC_SEQ = 1
PAST_LEN = 8192
PAGE_SIZE = 128

N_META = 16
BLOCK = 128
FOX_HEADS = 8
FOX_DH = 64
FOX_W = FOX_HEADS * FOX_DH
GLA_HEADS = 4
GLA_DK = 64
GLA_DV = 128
GLA_K = GLA_HEADS * GLA_DK
GLA_V = GLA_HEADS * GLA_DV
GLA_RANK = 16
GLA_TAU = 16.0
D_FF = ((-(-8 * D_MODEL // 3) + 255) // 256) * 256
LN_EPS = 1e-5
NORM_EPS = 1e-6
NEG_INF = -1e30
ALPHA = (2.0 * DEPTH) ** 0.25
BETA = (8.0 * DEPTH) ** -0.25
IN_SPLITS = (FOX_W, FOX_W, FOX_W, FOX_HEADS, GLA_K, GLA_K, GLA_V, GLA_V, GLA_RANK, D_MODEL, D_MODEL)
N_IN = sum(IN_SPLITS)

kernel_name = 'fox_gla_hybrid_step'


def _layer_norm(x, g, b):
    xf = x.astype(jnp.float32)
    mu = jnp.mean(xf, axis=-1, keepdims=True)
    var = jnp.mean(jnp.square(xf - mu), axis=-1, keepdims=True)
    y = (xf - mu) * lax.rsqrt(var + LN_EPS) * g.astype(jnp.float32) + b.astype(jnp.float32)
    return y.astype(x.dtype)


def _project_in(x, w_in, b_f, w_a2, b_a):
    lead = x.shape[:-1]
    idx = np.cumsum(IN_SPLITS)[:-1].tolist()
    qa, ka, va, fa, qb, kb, vb, rb, alr, ga, gb = jnp.split(x @ w_in, idx, axis=-1)
    heads = lambda t, h: t.reshape(*lead, h, t.shape[-1] // h)
    lf = jax.nn.log_sigmoid((fa + b_f).astype(jnp.float32))
    lg = jax.nn.log_sigmoid((alr @ w_a2 + b_a).astype(jnp.float32)) / GLA_TAU
    return (heads(qa, FOX_HEADS), heads(ka, FOX_HEADS), heads(va, FOX_HEADS), lf,
            heads(qb, GLA_HEADS) * (GLA_DK ** -0.5), heads(kb, GLA_HEADS), heads(vb, GLA_HEADS),
            heads(lg, GLA_HEADS), rb, ga, gb)


def _fox_prompt(q, k, v, lf):
    b, l, h, dh = q.shape
    pad = (-l) % BLOCK
    lp = l + pad
    nb = lp // BLOCK
    pw = ((0, 0), (pad, 0), (0, 0), (0, 0))
    q, k, v = jnp.pad(q, pw), jnp.pad(k, pw), jnp.pad(v, pw)
    c = jnp.cumsum(jnp.pad(lf, pw[:3]), axis=1)
    c_keys = c.transpose(0, 2, 1)
    pos = jnp.arange(lp)
    q_blocks = q.reshape(b, nb, BLOCK, h, dh).transpose(1, 0, 2, 3, 4)
    c_blocks = c.reshape(b, nb, BLOCK, h).transpose(1, 0, 3, 2)
    p_blocks = pos.reshape(nb, BLOCK)
    scale = dh ** -0.5

    def block(args):
        qi, ci, pi = args
        s = jnp.einsum('bqhd,bkhd->bhqk', qi, k).astype(jnp.float32) * scale
        s = s + ci[..., None] - c_keys[:, :, None, :]
        mask = (pos[None, :] <= pi[:, None]) & (pos[None, :] >= pad)
        s = jnp.where(mask, s, NEG_INF)
        p = jax.nn.softmax(s, axis=-1).astype(v.dtype)
        return jnp.einsum('bhqk,bkhd->bqhd', p, v)

    o = lax.map(block, (q_blocks, c_blocks, p_blocks))
    return o.transpose(1, 0, 2, 3, 4).reshape(b, lp, h, dh)[:, pad:]


def _fox_sample(q, k, v, lf, ck, cv, clf, page_table):
    b, t, h, dh = q.shape
    k_all = jnp.concatenate([ck[page_table].reshape(b, -1, h, dh), k.astype(ck.dtype)], axis=1)
    v_all = jnp.concatenate([cv[page_table].reshape(b, -1, h, dh), v.astype(cv.dtype)], axis=1)
    lf_all = jnp.concatenate([clf[page_table].reshape(b, -1, h).astype(jnp.float32), lf], axis=1)
    past = k_all.shape[1] - t
    c = jnp.cumsum(lf_all, axis=1).transpose(0, 2, 1)
    s = jnp.einsum('bqhd,bkhd->bhqk', q, k_all.astype(q.dtype)).astype(jnp.float32) * dh ** -0.5
    s = s + c[:, :, past:, None] - c[:, :, None, :]
    kpos = jnp.arange(past + t)
    qpos = past + jnp.arange(t)
    s = jnp.where(kpos[None, :] <= qpos[:, None], s, NEG_INF)
    p = jax.nn.softmax(s, axis=-1).astype(v_all.dtype)
    return jnp.einsum('bhqk,bkhd->bqhd', p, v_all)


def _gla_prompt(q, k, v, lg):
    b, l, h, dk = q.shape
    dv = v.shape[-1]
    pad = (-l) % BLOCK
    lp = l + pad
    nc = lp // BLOCK
    pw = ((0, 0), (pad, 0), (0, 0), (0, 0))

    def chunks(t):
        t = jnp.pad(t.astype(jnp.float32), pw)
        return t.reshape(b, nc, BLOCK, h, t.shape[-1]).transpose(1, 0, 3, 2, 4)

    causal = jnp.tril(jnp.ones((BLOCK, BLOCK), dtype=bool))

    def step(state, inp):
        qc, kc, vc, gc = inp
        bc = jnp.cumsum(gc, axis=2)
        diff = bc[:, :, :, None, :] - bc[:, :, None, :, :]
        decay = jnp.exp(jnp.where(causal[:, :, None], diff, -jnp.inf))
        scores = jnp.einsum('bhtsd,bhsd->bhts', qc[:, :, :, None, :] * decay, kc)
        o = (jnp.einsum('bhts,bhsv->bhtv', scores, vc)
             + jnp.einsum('bhtd,bhdv->bhtv', qc * jnp.exp(bc), state))
        b_last = bc[:, :, -1:, :]
        k_dec = kc * jnp.exp(b_last - bc)
        state = (jnp.exp(b_last[:, :, 0, :])[..., None] * state
                 + jnp.einsum('bhsd,bhsv->bhdv', k_dec, vc))
        return state, o

    s0 = jnp.zeros((b, h, dk, dv), jnp.float32)
    s_fin, o = lax.scan(step, s0, (chunks(q), chunks(k), chunks(v), chunks(lg)))
    o = o.transpose(1, 0, 3, 2, 4).reshape(b, lp, h, dv)[:, pad:]
    return o, s_fin


def _gla_sample(q, k, v, lg, state):
    def step(s, inp):
        qt, kt, vt, gt = inp
        s = jnp.exp(gt)[..., None] * s + kt[..., :, None] * vt[..., None, :]
        return s, jnp.einsum('bhd,bhdv->bhv', qt, s)

    tmaj = lambda t: t.astype(jnp.float32).transpose(1, 0, 2, 3)
    s_new, o = lax.scan(step, state.astype(jnp.float32), (tmaj(q), tmaj(k), tmaj(v), tmaj(lg)))
    return o.transpose(1, 0, 2, 3), s_new


def _merge(x, o_a, o_b, rb, ga, gb, gla_norm_g, w_pa, w_pb, w_o, g, b):
    lead = x.shape[:-1]
    ob = o_b.astype(jnp.float32)
    ob = ob * lax.rsqrt(jnp.mean(jnp.square(ob), axis=-1, keepdims=True) + NORM_EPS) * gla_norm_g.astype(jnp.float32)
    ob = (ob.reshape(*lead, GLA_V) * jax.nn.silu(rb.astype(jnp.float32))).astype(x.dtype)
    y_a = o_a.reshape(*lead, FOX_W).astype(x.dtype) @ w_pa
    y_b = ob @ w_pb
    mixed = jax.nn.sigmoid(ga) * y_a + jax.nn.sigmoid(gb) * y_b
    return _layer_norm(ALPHA * x + mixed @ w_o, g, b)


def _ffn(x, w_gate, w_up, w_down, g, b):
    hdn = jax.nn.silu(x @ w_gate) * (x @ w_up)
    return _layer_norm(ALPHA * x + hdn @ w_down, g, b)


def setup_inputs(seed: int = 0) -> dict:
    key = jax.random.key(seed)
    ks = jax.random.split(key, 24)
    n_pages = PAST_LEN // PAGE_SIZE
    n_used = DEC_BATCH * n_pages
    n_pool = n_used + max(1, n_used // 4)
    nrm = lambda k, shape, s: jax.random.normal(k, shape, jnp.float32) * s
    page_table = jax.random.permutation(ks[5], n_pool)[:n_used].reshape(DEC_BATCH, n_pages).astype(jnp.int32)
    return {
        'x_prompt': nrm(ks[0], (BATCH, SEQ, D_MODEL), 1.0),
        'x_sample': nrm(ks[1], (DEC_BATCH, DEC_SEQ, D_MODEL), 1.0),
        'cache_k': nrm(ks[2], (DEPTH, n_pool, PAGE_SIZE, FOX_HEADS, FOX_DH), 1.0),
        'cache_v': nrm(ks[3], (DEPTH, n_pool, PAGE_SIZE, FOX_HEADS, FOX_DH), 1.0),
        'cache_lf': jax.nn.log_sigmoid(nrm(ks[4], (DEPTH, n_pool, PAGE_SIZE, FOX_HEADS), 1.0)),
        'state_gla': nrm(ks[6], (DEPTH, DEC_BATCH, GLA_HEADS, GLA_DK, GLA_DV), 1.0),
        'page_table': page_table,
        'meta': nrm(ks[7], (N_META, D_MODEL), 1.0),
        'w_in': nrm(ks[8], (DEPTH, D_MODEL, N_IN), D_MODEL ** -0.5),
        'b_f': nrm(ks[9], (DEPTH, FOX_HEADS), 0.1),
        'w_a2': nrm(ks[10], (DEPTH, GLA_RANK, GLA_K), GLA_RANK ** -0.5),
        'b_a': nrm(ks[11], (DEPTH, GLA_K), 0.1),
        'gla_norm_g': 1.0 + nrm(ks[12], (DEPTH, GLA_DV), 0.1),
        'w_pa': nrm(ks[13], (DEPTH, FOX_W, D_MODEL), FOX_W ** -0.5),
        'w_pb': nrm(ks[14], (DEPTH, GLA_V, D_MODEL), GLA_V ** -0.5),
        'w_o': nrm(ks[15], (DEPTH, D_MODEL, D_MODEL), D_MODEL ** -0.5 * BETA),
        'ln1_g': 1.0 + nrm(ks[16], (DEPTH, D_MODEL), 0.1),
        'ln1_b': nrm(ks[17], (DEPTH, D_MODEL), 0.05),
        'w_gate': nrm(ks[18], (DEPTH, D_MODEL, D_FF), D_MODEL ** -0.5),
        'w_up': nrm(ks[19], (DEPTH, D_MODEL, D_FF), D_MODEL ** -0.5),
        'w_down': nrm(ks[20], (DEPTH, D_FF, D_MODEL), D_FF ** -0.5 * BETA),
        'ln2_g': 1.0 + nrm(ks[21], (DEPTH, D_MODEL), 0.1),
        'ln2_b': nrm(ks[22], (DEPTH, D_MODEL), 0.05),
    }


def reference(x_prompt, x_sample, cache_k, cache_v, cache_lf, state_gla, page_table,
              meta, w_in, b_f, w_a2, b_a, gla_norm_g, w_pa, w_pb, w_o, ln1_g, ln1_b,
              w_gate, w_up, w_down, ln2_g, ln2_b):
    bsz = x_prompt.shape[0]
    meta_b = jnp.broadcast_to(meta[None].astype(x_prompt.dtype), (bsz, N_META, D_MODEL))
    hp = jnp.concatenate([meta_b, x_prompt], axis=1)
    hs = x_sample
    kp_rows, vp_rows, lfp_rows, gp_states = [], [], [], []
    ks_rows, vs_rows, lfs_rows, gs_states = [], [], [], []
    for l in range(DEPTH):
        qa, ka, va, lf, qb, kb, vb, lg, rb, ga, gb = _project_in(hp, w_in[l], b_f[l], w_a2[l], b_a[l])
        o_a = _fox_prompt(qa, ka, va, lf)
        o_b, s_fin = _gla_prompt(qb, kb, vb, lg)
        hp = _merge(hp, o_a, o_b, rb, ga, gb, gla_norm_g[l], w_pa[l], w_pb[l], w_o[l], ln1_g[l], ln1_b[l])
        hp = _ffn(hp, w_gate[l], w_up[l], w_down[l], ln2_g[l], ln2_b[l])
        kp_rows.append(ka)
        vp_rows.append(va)
        lfp_rows.append(lf)
        gp_states.append(s_fin)
        qa, ka, va, lf, qb, kb, vb, lg, rb, ga, gb = _project_in(hs, w_in[l], b_f[l], w_a2[l], b_a[l])
        o_a = _fox_sample(qa, ka, va, lf, cache_k[l], cache_v[l], cache_lf[l], page_table)
        o_b, s_new = _gla_sample(qb, kb, vb, lg, state_gla[l])
        hs = _merge(hs, o_a, o_b, rb, ga, gb, gla_norm_g[l], w_pa[l], w_pb[l], w_o[l], ln1_g[l], ln1_b[l])
        hs = _ffn(hs, w_gate[l], w_up[l], w_down[l], ln2_g[l], ln2_b[l])
        ks_rows.append(ka)
        vs_rows.append(va)
        lfs_rows.append(lf)
        gs_states.append(s_new)
    y_prompt = hp[:, N_META:]
    y_sample = hs
    return (y_prompt, y_sample,
            jnp.stack(kp_rows), jnp.stack(vp_rows), jnp.stack(lfp_rows), jnp.stack(gp_states),
            jnp.stack(ks_rows), jnp.stack(vs_rows), jnp.stack(lfs_rows), jnp.stack(gs_states))
```

```python
import functools

import numpy as np
import jax
import jax.numpy as jnp
from jax import lax
from jax.experimental import pallas as pl
from jax.experimental.pallas import tpu as pltpu

D_MODEL = 1024
BATCH = 2
SEQ = 8192
DEPTH = 2
DEC_BATCH = 32
PAST_LEN = 8192
PAGE_SIZE = 128
N_META = 16
FOX_HEADS = 8
FOX_DH = 64
FOX_W = FOX_HEADS * FOX_DH
GLA_HEADS = 4
GLA_DK = 64
GLA_DV = 128
GLA_K = GLA_HEADS * GLA_DK
GLA_V = GLA_HEADS * GLA_DV
GLA_RANK = 16
GLA_TAU = 16.0
D_FF = 2816
LN_EPS = 1e-5
NORM_EPS = 1e-6
NEG_INF = -1e30
ALPHA = (2.0 * DEPTH) ** 0.25
_SPLITS = (FOX_W, FOX_W, FOX_W, FOX_HEADS, GLA_K, GLA_K, GLA_V, GLA_V, GLA_RANK, D_MODEL, D_MODEL)
_OFF = np.concatenate([[0], np.cumsum(_SPLITS)]).tolist()

LANES = 128
SUBLANES = 8
VMEM_LIMIT = 56 * 1024 * 1024

L_REAL = SEQ + N_META
FOX_BLK = 256
LP = -(-L_REAL // FOX_BLK) * FOX_BLK
PADF = LP - L_REAL
GLA_CHUNK = 128
GLA_TILE = 256
TM_IN = 384
CUM_BLK = 128
TM_TOK = 512
N_PAGES = PAST_LEN // PAGE_SIZE
PAGES_PER_STEP = 8

F32 = jnp.float32
BF16 = jnp.bfloat16


def _dot(a, b):
    return jnp.dot(a, b, preferred_element_type=F32)


def _dot_nt(a, b):
    return lax.dot_general(a, b, (((1,), (1,)), ((), ())), preferred_element_type=F32)


def _dot_tn(a, b):
    return lax.dot_general(a, b, (((0,), (0,)), ((), ())), preferred_element_type=F32)


def _log_sigmoid(x):
    return jnp.minimum(x, 0.0) - jnp.log(1.0 + jnp.exp(-jnp.abs(x)))


def _sigmoid(x):
    return 1.0 / (1.0 + jnp.exp(-x))


def _layer_norm(y, g, b):
    mu = jnp.mean(y, axis=-1, keepdims=True)
    d = y - mu
    var = jnp.mean(d * d, axis=-1, keepdims=True)
    return d * lax.rsqrt(var + LN_EPS) * g + b


def _split3(x):
    hi = x.astype(BF16)
    r = x - hi.astype(F32)
    mid = r.astype(BF16)
    lo = (r - mid.astype(F32)).astype(BF16)
    return hi, mid, lo


def _inproj_kernel(x_ref, wqkv_ref, wg_ref, ws_ref, w2_ref, bs_ref, ba_ref, tri_ref,
                   q_ref, k_ref, v_ref, kf_ref, vf_ref, lf_ref, gq_ref, gk_ref, gv_ref, lg_ref,
                   carry_ref, *, tm):
    i = pl.program_id(1)

    @pl.when(i == 0)
    def _():
        carry_ref[...] = jnp.zeros_like(carry_ref)

    row = i * tm + lax.broadcasted_iota(jnp.int32, (tm, 1), 0)
    real = row >= PADF
    xb = jnp.where(real, x_ref[0], 0.0).astype(BF16)

    rs = _dot(xb, ws_ref[...]) + bs_ref[...]
    lf_full = jnp.where(real, _log_sigmoid(rs), 0.0)
    lf_ref[0] = lf_full[:, :FOX_HEADS]
    z = _dot(rs.astype(BF16), w2_ref[...]) + ba_ref[...]
    lg_ref[0] = _log_sigmoid(z) * (1.0 / GLA_TAU)

    carry = carry_ref[...]
    tri = tri_ref[...]
    cs = []
    for sb in range(tm // CUM_BLK):
        hi, mid, lo = _split3(lf_full[sb * CUM_BLK:(sb + 1) * CUM_BLK])
        c = _dot(tri, hi) + _dot(tri, mid) + _dot(tri, lo) + carry
        carry = c[CUM_BLK - 1:CUM_BLK]
        cs.append(c)
    carry_ref[...] = carry
    c = jnp.concatenate(cs, axis=0)

    r = _dot(xb, wqkv_ref[...])
    kf_ref[0] = r[:, FOX_W:2 * FOX_W]
    vf_ref[0] = r[:, 2 * FOX_W:3 * FOX_W]

    lane = lax.broadcasted_iota(jnp.int32, (1, LANES), 1)
    for h in range(FOX_HEADS):
        p, e = divmod(h, 2)
        dmask = (lane < FOX_DH) if e == 0 else (lane >= FOX_DH)
        xo = FOX_DH if e == 0 else 0
        ch = jnp.broadcast_to(c[:, h:h + 1], (tm, LANES))
        hi = ch.astype(BF16).astype(F32)
        r1 = ch - hi
        mid = r1.astype(BF16).astype(F32)
        lo = r1 - mid
        one3 = (lane >= xo + 3) & (lane < xo + 6)
        eq = jnp.where(lane == xo, hi, jnp.where(lane == xo + 1, mid, jnp.where(
            lane == xo + 2, lo, jnp.where(one3, 1.0, 0.0))))
        first3 = (lane >= xo) & (lane < xo + 3)
        ek = jnp.where(first3, 1.0, jnp.where(lane == xo + 3, -hi, jnp.where(
            lane == xo + 4, -mid, jnp.where(lane == xo + 5, -lo, 0.0))))
        ev = jnp.where(lane == xo, 1.0, 0.0)
        rq = r[:, p * LANES:(p + 1) * LANES]
        rk = r[:, FOX_W + p * LANES:FOX_W + (p + 1) * LANES]
        rv = r[:, 2 * FOX_W + p * LANES:2 * FOX_W + (p + 1) * LANES]
        q_ref[0, h] = jnp.where(dmask, rq, eq).astype(BF16)
        k_ref[0, h] = jnp.where(dmask, rk, ek).astype(BF16)
        v_ref[0, h] = jnp.where(dmask, rv, ev).astype(BF16)

    rg = _dot(xb, wg_ref[...])
    gq_ref[0] = rg[:, :GLA_K]
    gk_ref[0] = rg[:, GLA_K:2 * GLA_K]
    gv_ref[0] = rg[:, 2 * GLA_K:].astype(BF16)


def _const_spec(shape):
    nd = len(shape)
    return pl.BlockSpec(shape, lambda *_: (0,) * nd, pipeline_mode=pl.Buffered(1))


def _inproj(hp, lw):
    tm = TM_IN
    nt = LP // tm
    tri = jnp.asarray(np.tril(np.ones((CUM_BLK, CUM_BLK), np.float32)), BF16)
    row3 = lambda w: pl.BlockSpec((1, tm, w), lambda b, i: (b, i, 0))
    head4 = pl.BlockSpec((1, FOX_HEADS, tm, LANES), lambda b, i: (b, 0, i, 0))
    out_shape = (
        jax.ShapeDtypeStruct((BATCH, FOX_HEADS, LP, LANES), BF16),
        jax.ShapeDtypeStruct((BATCH, FOX_HEADS, LP, LANES), BF16),
        jax.ShapeDtypeStruct((BATCH, FOX_HEADS, LP, LANES), BF16),
        jax.ShapeDtypeStruct((BATCH, LP, FOX_W), F32),
        jax.ShapeDtypeStruct((BATCH, LP, FOX_W), F32),
        jax.ShapeDtypeStruct((BATCH, LP, FOX_HEADS), F32),
        jax.ShapeDtypeStruct((BATCH, LP, GLA_K), F32),
        jax.ShapeDtypeStruct((BATCH, LP, GLA_K), F32),
        jax.ShapeDtypeStruct((BATCH, LP, GLA_V), BF16),
        jax.ShapeDtypeStruct((BATCH, LP, GLA_K), F32),
    )
    return pl.pallas_call(
        functools.partial(_inproj_kernel, tm=tm),
        grid=(BATCH, nt),
        in_specs=[row3(D_MODEL), _const_spec(lw["wqkv"].shape), _const_spec(lw["wgla"].shape),
                  _const_spec(lw["ws"].shape), _const_spec(lw["w2"].shape),
                  _const_spec(lw["bs"].shape), _const_spec(lw["ba"].shape), _const_spec(tri.shape)],
        out_specs=(head4, head4, head4, row3(FOX_W), row3(FOX_W), row3(FOX_HEADS),
                   row3(GLA_K), row3(GLA_K), row3(GLA_V), row3(GLA_K)),
        out_shape=out_shape,
        scratch_shapes=[pltpu.VMEM((1, LANES), F32)],
        compiler_params=pltpu.CompilerParams(
            dimension_semantics=("parallel", "arbitrary"), vmem_limit_bytes=VMEM_LIMIT),
        name="prompt_inproj",
    )(hp, lw["wqkv"], lw["wgla"], lw["ws"], lw["w2"], lw["bs"], lw["ba"], tri)


def _fox_kernel(q_ref, k_ref, v_ref, o_ref, vt_sc, acc_sc, m_sc):
    i = pl.program_id(2)
    blk = FOX_BLK
    half = blk // 2

    @pl.when(i == 0)
    def _():
        for e in range(2):
            def body(c, carry):
                for s in range(2):
                    t = v_ref[0, e, pl.ds(c * blk + s * half, half), :].astype(F32)
                    vt_sc[e, c, :, s * half:(s + 1) * half] = t.T.astype(BF16)
                return carry
            lax.fori_loop(0, LP // blk, body, 0)

    for e in range(2):
        m_sc[e] = jnp.full((1, blk), NEG_INF, F32)
        acc_sc[e] = jnp.zeros((LANES, blk), F32)

    qpos = i * blk + lax.broadcasted_iota(jnp.int32, (1, blk), 1)

    def block(j, masked):
        for e in range(2):
            start = j * blk if isinstance(j, int) else pl.multiple_of(j * blk, blk)
            kb = k_ref[0, e, pl.ds(start, blk), :]
            st = _dot_nt(kb, q_ref[0, e])
            if masked:
                kpos = j * blk + lax.broadcasted_iota(jnp.int32, (blk, 1), 0)
                st = jnp.where((kpos <= qpos) & (kpos >= PADF), st, NEG_INF)
            m_old = m_sc[e]
            m_new = jnp.maximum(m_old, jnp.max(st, axis=0, keepdims=True))
            alpha = jnp.exp(m_old - m_new)
            pt = jnp.exp(st - m_new).astype(BF16)
            acc_sc[e] = alpha * acc_sc[e] + _dot(vt_sc[e, j], pt)
            m_sc[e] = m_new

    block(0, True)

    def body(j, carry):
        block(j, False)
        return carry
    lax.fori_loop(1, i, body, 0)

    @pl.when(i > 0)
    def _():
        block(i, True)

    a0 = acc_sc[0]
    a1 = acc_sc[1]
    row = lax.broadcasted_iota(jnp.int32, (LANES, 1), 0)
    ot = jnp.where(row < FOX_DH, a0 / a0[FOX_DH:FOX_DH + 1, :], a1 / a1[0:1, :])
    o_ref[0] = ot.T.astype(BF16)


def _fox_prompt(q, k, v):
    nq = LP // FOX_BLK
    return pl.pallas_call(
        _fox_kernel,
        grid=(BATCH, FOX_HEADS // 2, nq),
        in_specs=[pl.BlockSpec((1, 2, FOX_BLK, LANES), lambda b, p, i: (b, p, i, 0)),
                  pl.BlockSpec((1, 2, LP, LANES), lambda b, p, i: (b, p, 0, 0)),
                  pl.BlockSpec((1, 2, LP, LANES), lambda b, p, i: (b, p, 0, 0))],
        out_specs=pl.BlockSpec((1, FOX_BLK, LANES), lambda b, p, i: (b, i, p)),
        out_shape=jax.ShapeDtypeStruct((BATCH, LP, FOX_W), BF16),
        scratch_shapes=[pltpu.VMEM((2, nq, LANES, FOX_BLK), BF16),
                        pltpu.VMEM((2, LANES, FOX_BLK), F32),
                        pltpu.VMEM((2, 1, FOX_BLK), F32)],
        compiler_params=pltpu.CompilerParams(
            dimension_semantics=("parallel", "parallel", "arbitrary"), vmem_limit_bytes=VMEM_LIMIT),
        name="fox_prompt",
    )(q, k, v)


def _gla_tables():
    c = GLA_CHUNK
    t = np.arange(c)[:, None]
    j = np.arange(c)[None, :]
    mats = [(j <= t).astype(np.float32), (j > t).astype(np.float32)]
    masks = [np.eye(c, dtype=np.float32)]
    blk = c
    while blk >= 2:
        half = blk // 2
        mid = (t // blk) * blk + half
        mats.append((j <= t).astype(np.float32) - (j <= mid).astype(np.float32))
        s = j
        masks.append((((t // blk) == (s // blk)) & ((t % blk) >= half) & ((s % blk) < half))
                     .astype(np.float32))
        blk = half
    return np.concatenate(mats, axis=0), np.stack(masks, axis=0)


_GLA_LEVELS = 7


def _gla_kernel(q_ref, k_ref, lg_ref, v_ref, dall_ref, masks_ref, o_ref, sfin_ref, st_sc):
    i = pl.program_id(2)
    c = GLA_CHUNK

    @pl.when(i == 0)
    def _():
        st_sc[...] = jnp.zeros_like(st_sc)

    lane = lax.broadcasted_iota(jnp.int32, (1, LANES), 1)
    hmask = (lane < GLA_DK, lane >= GLA_DK)

    for ci in range(GLA_TILE // c):
        rows = slice(ci * c, (ci + 1) * c)
        q = q_ref[0, rows, :]
        k = k_ref[0, rows, :]
        lg = lg_ref[0, rows, :]
        hi = lg.astype(BF16)
        lo = (lg - hi.astype(F32)).astype(BF16)
        e2 = _dot(dall_ref[...], jnp.concatenate([hi, lo], axis=1))
        ex = e2[:, :LANES] + e2[:, LANES:]
        bc = ex[0:c]
        q_in = q * jnp.exp(bc)
        k_dec = k * jnp.exp(ex[c:2 * c])
        qs = [q]
        ks = [k.astype(BF16)]
        for lv in range(_GLA_LEVELS):
            f = jnp.exp(-jnp.abs(ex[(2 + lv) * c:(3 + lv) * c]))
            qs.append(q * f)
            ks.append((k * f).astype(BF16))
        decay_all = jnp.exp(bc[c - 1:c])
        for h in range(2):
            a = jnp.zeros((c, c), F32)
            for lv in range(_GLA_LEVELS + 1):
                ql = jnp.where(hmask[h], qs[lv], 0.0).astype(BF16)
                a = a + masks_ref[lv] * _dot_nt(ql, ks[lv])
            vh = v_ref[0, rows, h * GLA_DV:(h + 1) * GLA_DV]
            st = st_sc[h]
            o = _dot(a.astype(BF16), vh) + _dot_nt(jnp.where(hmask[h], q_in, 0.0).astype(BF16),
                                                   st.astype(BF16))
            o_ref[0, rows, h * GLA_DV:(h + 1) * GLA_DV] = o
            kd = jnp.where(hmask[h], k_dec, 0.0).astype(BF16)
            st_sc[h] = decay_all * st + _dot_tn(vh, kd)

    @pl.when(i == pl.num_programs(2) - 1)
    def _():
        for h in range(2):
            s = st_sc[h].T
            sfin_ref[0, h] = s[h * GLA_DK:(h + 1) * GLA_DK, :]


def _gla_prompt(gq, gk, lg, gv):
    dall_np, masks_np = _gla_tables()
    dall = jnp.asarray(dall_np, BF16)
    masks = jnp.asarray(masks_np, F32)
    t = GLA_TILE
    pair = pl.BlockSpec((1, t, LANES), lambda b, p, i: (b, i, p))
    wide = pl.BlockSpec((1, t, 2 * GLA_DV), lambda b, p, i: (b, i, p))
    return pl.pallas_call(
        _gla_kernel,
        grid=(BATCH, GLA_HEADS // 2, LP // t),
        in_specs=[pair, pair, pair, wide, _const_spec(dall.shape), _const_spec(masks.shape)],
        out_specs=(wide, pl.BlockSpec((1, 2, GLA_DK, GLA_DV), lambda b, p, i: (b, p, 0, 0))),
        out_shape=(jax.ShapeDtypeStruct((BATCH, LP, GLA_V), F32),
                   jax.ShapeDtypeStruct((BATCH, GLA_HEADS, GLA_DK, GLA_DV), F32)),
        scratch_shapes=[pltpu.VMEM((2, GLA_DV, LANES), F32)],
        compiler_params=pltpu.CompilerParams(
            dimension_semantics=("parallel", "parallel", "arbitrary"), vmem_limit_bytes=VMEM_LIMIT),
        name="gla_prompt",
    )(gq, gk, lg, gv, dall, masks)


def _merge_kernel(h_ref, oa_ref, ob_ref, wg_ref, wpa_ref, wpb_ref, wo_ref, gn_ref, g_ref, b_ref,
                  out_ref):
    x = h_ref[...]
    xb = x.astype(BF16)
    r = _dot(xb, wg_ref[...])
    rb = r[:, :GLA_V]
    ga = r[:, GLA_V:GLA_V + D_MODEL]
    gb = r[:, GLA_V + D_MODEL:]
    ob = ob_ref[...]
    parts = []
    for hd in range(GLA_HEADS):
        o = ob[:, hd * GLA_DV:(hd + 1) * GLA_DV]
        ms = jnp.mean(o * o, axis=-1, keepdims=True)
        parts.append(o * lax.rsqrt(ms + NORM_EPS) * gn_ref[...])
    obn = jnp.concatenate(parts, axis=1) * (rb * _sigmoid(rb))
    y_a = _dot(oa_ref[...], wpa_ref[...])
    y_b = _dot(obn.astype(BF16), wpb_ref[...])
    mixed = _sigmoid(ga) * y_a + _sigmoid(gb) * y_b
    y = ALPHA * x + _dot(mixed.astype(BF16), wo_ref[...])
    out_ref[...] = _layer_norm(y, g_ref[...], b_ref[...])


def _merge(h, oa, ob, lw, tm):
    m = h.shape[0]
    row = lambda w: pl.BlockSpec((tm, w), lambda i: (i, 0))
    ws = (lw["wmg"], lw["wpa"], lw["wpb"], lw["wo"], lw["gn"], lw["ln1g"], lw["ln1b"])
    return pl.pallas_call(
        _merge_kernel,
        grid=(m // tm,),
        in_specs=[row(D_MODEL), row(FOX_W), row(GLA_V)] + [_const_spec(w.shape) for w in ws],
        out_specs=row(D_MODEL),
        out_shape=jax.ShapeDtypeStruct((m, D_MODEL), F32),
        compiler_params=pltpu.CompilerParams(
            dimension_semantics=("parallel",), vmem_limit_bytes=VMEM_LIMIT),
        name="merge",
    )(h, oa, ob, *ws)


def _ffn_kernel(h_ref, wgate_ref, wup_ref, wdown_ref, g_ref, b_ref, out_ref):
    x = h_ref[...]
    xb = x.astype(BF16)
    gt = _dot(xb, wgate_ref[...])
    up = _dot(xb, wup_ref[...])
    hdn = (gt * _sigmoid(gt) * up).astype(BF16)
    y = ALPHA * x + _dot(hdn, wdown_ref[...])
    out_ref[...] = _layer_norm(y, g_ref[...], b_ref[...])


def _ffn(h, lw, tm):
    m = h.shape[0]
    row = pl.BlockSpec((tm, D_MODEL), lambda i: (i, 0))
    ws = (lw["wgate"], lw["wup"], lw["wdown"], lw["ln2g"], lw["ln2b"])
    return pl.pallas_call(
        _ffn_kernel,
        grid=(m // tm,),
        in_specs=[row] + [_const_spec(w.shape) for w in ws],
        out_specs=row,
        out_shape=jax.ShapeDtypeStruct((m, D_MODEL), F32),
        compiler_params=pltpu.CompilerParams(
            dimension_semantics=("parallel",), vmem_limit_bytes=VMEM_LIMIT),
        name="ffn",
    )(h, *ws)


def _sample_inproj_kernel(x_ref, wqkv_ref, wg_ref, ws_ref, w2_ref, bs_ref, ba_ref,
                          r1_ref, r2_ref, lf_ref, lg_ref):
    xb = x_ref[...].astype(BF16)
    r1_ref[...] = _dot(xb, wqkv_ref[...])
    r2_ref[...] = _dot(xb, wg_ref[...])
    rs = _dot(xb, ws_ref[...]) + bs_ref[...]
    lf_ref[...] = _log_sigmoid(rs)
    z = _dot(rs.astype(BF16), w2_ref[...]) + ba_ref[...]
    lg_ref[...] = _log_sigmoid(z) * (1.0 / GLA_TAU)


def _sample_inproj(x, lw):
    ws = (lw["wqkv"], lw["wgla"], lw["ws"], lw["w2"], lw["bs"], lw["ba"])
    full = lambda shape: pl.BlockSpec(shape, lambda i: (0,) * len(shape))
    out_shape = (jax.ShapeDtypeStruct((DEC_BATCH, 3 * FOX_W), F32),
                 jax.ShapeDtypeStruct((DEC_BATCH, 2 * GLA_K + GLA_V), F32),
                 jax.ShapeDtypeStruct((DEC_BATCH, LANES), F32),
                 jax.ShapeDtypeStruct((DEC_BATCH, GLA_K), F32))
    return pl.pallas_call(
        _sample_inproj_kernel,
        grid=(1,),
        in_specs=[full(x.shape)] + [full(w.shape) for w in ws],
        out_specs=tuple(full(s.shape) for s in out_shape),
        out_shape=out_shape,
        compiler_params=pltpu.CompilerParams(vmem_limit_bytes=VMEM_LIMIT),
        name="sample_inproj",
    )(x, *ws)


def _decode_tables():
    j = np.arange(LANES)
    same = (j[:, None] % FOX_HEADS) == (j[None, :] % FOX_HEADS)
    ut = np.concatenate([(same & (j[:, None] > j[None, :])), same], axis=1).astype(np.float32)
    nr = PAGES_PER_STEP * SUBLANES
    r = np.arange(nr)
    us = np.concatenate([(r[None, :] > r[:, None]), np.ones((nr, nr), bool)], axis=0)
    return ut, us.astype(np.float32)


def _fox_decode_kernel(pt_ref, qt_ref, kn_ref, vn_ref, lfn_ref, ut_ref, us_ref, *rest):
    npg = PAGES_PER_STEP
    kp = rest[0:npg]
    vp = rest[npg:2 * npg]
    lp = rest[2 * npg:3 * npg]
    o_ref = rest[3 * npg]
    m_sc, l_sc, acc_sc, carry_sc = rest[3 * npg + 1:]
    del pt_ref
    j = pl.program_id(1)
    nr = npg * SUBLANES
    rows_pp = PAGE_SIZE * FOX_HEADS

    qt = qt_ref[0].astype(BF16)
    lane8 = lax.broadcasted_iota(jnp.int32, (SUBLANES, LANES), 1)
    sub8 = lax.broadcasted_iota(jnp.int32, (SUBLANES, LANES), 0)
    hsel = (lane8 % FOX_HEADS) == sub8
    dsel = lane8 == sub8
    eye = (lax.broadcasted_iota(jnp.int32, (LANES, LANES), 0)
           == lax.broadcasted_iota(jnp.int32, (LANES, LANES), 1))

    def fold(x, op):
        x = jnp.broadcast_to(x, (SUBLANES, LANES))
        for sh in (8, 16, 32, 64):
            x = op(x, pltpu.roll(x, sh, axis=1))
        return x[0:1]

    @pl.when(j == 0)
    def _():
        s = _dot(kn_ref[0].astype(BF16), qt)
        m_sc[...] = jnp.sum(jnp.where(hsel, s, 0.0), axis=0, keepdims=True)
        l_sc[...] = jnp.ones_like(l_sc)
        acc_sc[...] = vn_ref[0].astype(BF16).astype(F32)
        carry_sc[...] = lfn_ref[0]

    srows = []
    for g in range(npg):
        k2 = kp[g][...].reshape(rows_pp, FOX_DH).astype(BF16)
        s3 = _dot(k2, qt).reshape(SUBLANES, LANES, LANES)
        srows.append(jnp.sum(jnp.where(eye[None], s3, 0.0), axis=1))
    s = jnp.concatenate(srows, axis=0)

    lfc = jnp.concatenate([lp[g][...] for g in range(npg)], axis=0)
    w = _dot(jnp.concatenate(_split3(lfc), axis=0), ut_ref[...])
    wsum = w[0:nr] + w[nr:2 * nr] + w[2 * nr:3 * nr]
    tot_rows = wsum[:, LANES:]
    x = _dot(us_ref[...], jnp.concatenate(_split3(tot_rows), axis=1))
    xs = x[:, :LANES] + x[:, LANES:2 * LANES] + x[:, 2 * LANES:]
    carry = carry_sc[...]
    s = s + (wsum[:, :LANES] + xs[0:nr] + carry)
    carry_sc[...] = carry + xs[nr:nr + 1]

    m_old = m_sc[...]
    m_new = jnp.maximum(m_old, fold(jnp.max(s, axis=0, keepdims=True), jnp.maximum))
    alpha = jnp.exp(m_old - m_new)
    p = jnp.exp(s - m_new)
    l_sc[...] = alpha * l_sc[...] + fold(jnp.sum(p, axis=0, keepdims=True), jnp.add)
    m_sc[...] = m_new
    acol = jnp.sum(jnp.where(dsel, jnp.broadcast_to(alpha, (SUBLANES, LANES)), 0.0),
                   axis=1, keepdims=True)

    pv = jnp.zeros((FOX_HEADS, FOX_DH), F32)
    for g in range(npg):
        pm = jnp.concatenate(
            [jnp.where(hsel, jnp.broadcast_to(p[g * SUBLANES + b:g * SUBLANES + b + 1],
                                              (SUBLANES, LANES)), 0.0)
             for b in range(SUBLANES)], axis=1).astype(BF16)
        v2 = vp[g][...].reshape(rows_pp, FOX_DH).astype(BF16)
        pv = pv + _dot(pm, v2)
    acc_sc[...] = acol * acc_sc[...] + pv

    @pl.when(j == pl.num_programs(1) - 1)
    def _():
        lcol = jnp.sum(jnp.where(dsel, jnp.broadcast_to(l_sc[...], (SUBLANES, LANES)), 0.0),
                       axis=1, keepdims=True)
        o_ref[0] = acc_sc[...] / lcol


def _fox_decode(layer, page_table, qt, kn, vn, lfn, cache_k, cache_v, cache_lf_t):
    npg = PAGES_PER_STEP
    nsteps = N_PAGES // npg
    ut_np, us_np = _decode_tables()
    ut = jnp.asarray(ut_np, BF16)
    us = jnp.asarray(us_np, BF16)

    def page_map(g, tail):
        def f(b, j, pt):
            return (layer, pt[b, (nsteps - 1 - j) * npg + g]) + tail
        return f

    kv_specs = [pl.BlockSpec((None, None, PAGE_SIZE, FOX_HEADS, FOX_DH), page_map(g, (0, 0, 0)))
                for g in range(npg)]
    lf_specs = [pl.BlockSpec((None, None, SUBLANES, LANES), page_map(g, (0, 0)))
                for g in range(npg)]
    per_b = lambda shape: pl.BlockSpec((1,) + shape, lambda b, j, pt: (b,) + (0,) * len(shape))
    const = lambda shape: pl.BlockSpec(shape, lambda b, j, pt: (0,) * len(shape))
    grid_spec = pltpu.PrefetchScalarGridSpec(
        num_scalar_prefetch=1,
        grid=(DEC_BATCH, nsteps),
        in_specs=[per_b((FOX_DH, LANES)), per_b((FOX_HEADS, FOX_DH)), per_b((FOX_HEADS, FOX_DH)),
                  per_b((1, LANES)), const(ut.shape), const(us.shape)]
        + kv_specs + kv_specs + lf_specs,
        out_specs=per_b((FOX_HEADS, FOX_DH)),
        scratch_shapes=[pltpu.VMEM((1, LANES), F32), pltpu.VMEM((1, LANES), F32),
                        pltpu.VMEM((FOX_HEADS, FOX_DH), F32), pltpu.VMEM((1, LANES), F32)],
    )
    return pl.pallas_call(
        _fox_decode_kernel,
        grid_spec=grid_spec,
        out_shape=jax.ShapeDtypeStruct((DEC_BATCH, FOX_HEADS, FOX_DH), F32),
        compiler_params=pltpu.CompilerParams(
            dimension_semantics=("parallel", "arbitrary"), vmem_limit_bytes=VMEM_LIMIT),
        name="fox_decode",
    )(page_table, qt, kn, vn, lfn, ut, us,
      *([cache_k] * npg), *([cache_v] * npg), *([cache_lf_t] * npg))


def _gla_decode_kernel(q_ref, k_ref, g_ref, v_ref, s_ref, o_ref, sn_ref):
    eye = (lax.broadcasted_iota(jnp.int32, (GLA_DK, GLA_DK), 0)
           == lax.broadcasted_iota(jnp.int32, (GLA_DK, GLA_DK), 1))

    def col(r):
        return jnp.sum(jnp.where(eye, jnp.broadcast_to(r, (GLA_DK, GLA_DK)), 0.0),
                       axis=1, keepdims=True)

    for h in range(GLA_HEADS):
        qc = col(q_ref[0, h:h + 1, :])
        kc = col(k_ref[0, h:h + 1, :])
        ac = col(jnp.exp(g_ref[0, h:h + 1, :]))
        sn = ac * s_ref[0, h] + kc * v_ref[0, h:h + 1, :]
        sn_ref[0, h] = sn
        o_ref[0, h:h + 1, :] = jnp.sum(qc * sn, axis=0, keepdims=True)


def _gla_decode(gq, gk, lg, gv, state):
    hk = pl.BlockSpec((1, GLA_HEADS, GLA_DK), lambda b: (b, 0, 0))
    hv = pl.BlockSpec((1, GLA_HEADS, GLA_DV), lambda b: (b, 0, 0))
    st = pl.BlockSpec((1, GLA_HEADS, GLA_DK, GLA_DV), lambda b: (b, 0, 0, 0))
    return pl.pallas_call(
        _gla_decode_kernel,
        grid=(DEC_BATCH,),
        in_specs=[hk, hk, hk, hv, st],
        out_specs=(hv, st),
        out_shape=(jax.ShapeDtypeStruct((DEC_BATCH, GLA_HEADS, GLA_DV), F32),
                   jax.ShapeDtypeStruct((DEC_BATCH, GLA_HEADS, GLA_DK, GLA_DV), F32)),
        compiler_params=pltpu.CompilerParams(
            dimension_semantics=("parallel",), vmem_limit_bytes=VMEM_LIMIT),
        name="gla_decode",
    )(gq, gk, lg, gv, state)


def _layer_weights(l, w_in, b_f, w_a2, b_a, gla_norm_g, w_pa, w_pb, w_o, ln1_g, ln1_b,
                   w_gate, w_up, w_down, ln2_g, ln2_b):
    w = w_in[l]
    o = _OFF
    sc_f = FOX_DH ** -0.5
    sc_g = GLA_DK ** -0.5
    seg = lambda a: w[:, o[a]:o[a + 1]]
    wqkv = jnp.concatenate([seg(0) * sc_f, seg(1), seg(2)], axis=1).astype(BF16)
    wgla = jnp.concatenate([seg(4) * sc_g, seg(5), seg(6)], axis=1).astype(BF16)
    ws = jnp.zeros((D_MODEL, LANES), F32)
    ws = ws.at[:, :FOX_HEADS].set(seg(3)).at[:, FOX_HEADS:FOX_HEADS + GLA_RANK].set(seg(8))
    w2 = jnp.zeros((LANES, GLA_K), F32).at[FOX_HEADS:FOX_HEADS + GLA_RANK].set(w_a2[l])
    bs = jnp.zeros((1, LANES), F32).at[0, :FOX_HEADS].set(b_f[l])
    wmg = jnp.concatenate([seg(7), seg(9), seg(10)], axis=1).astype(BF16)
    return dict(
        wqkv=wqkv, wgla=wgla, ws=ws.astype(BF16), w2=w2.astype(BF16), bs=bs, ba=b_a[l][None],
        wmg=wmg, wpa=w_pa[l].astype(BF16), wpb=w_pb[l].astype(BF16), wo=w_o[l].astype(BF16),
        gn=gla_norm_g[l][None], ln1g=ln1_g[l][None], ln1b=ln1_b[l][None],
        wgate=w_gate[l].astype(BF16), wup=w_up[l].astype(BF16), wdown=w_down[l].astype(BF16),
        ln2g=ln2_g[l][None], ln2b=ln2_b[l][None])


def kernel(x_prompt, x_sample, cache_k, cache_v, cache_lf, state_gla, page_table, meta, w_in, b_f,
           w_a2, b_a, gla_norm_g, w_pa, w_pb, w_o, ln1_g, ln1_b, w_gate, w_up, w_down, ln2_g, ln2_b):
    assert x_prompt.shape == (BATCH, SEQ, D_MODEL) and x_sample.shape == (DEC_BATCH, 1, D_MODEL)
    front = jnp.concatenate([jnp.zeros((PADF, D_MODEL), F32), meta.astype(F32)], axis=0)
    hp = jnp.concatenate([jnp.broadcast_to(front[None], (BATCH, PADF + N_META, D_MODEL)), x_prompt],
                         axis=1)
    hs = x_sample.reshape(DEC_BATCH, D_MODEL)
    n_pool = cache_k.shape[1]
    cache_lf_t = cache_lf.reshape(DEPTH, n_pool, SUBLANES, LANES)

    kp, vp, lfp, gp, ksr, vsr, lfs, gs = [], [], [], [], [], [], [], []
    for l in range(DEPTH):
        lw = _layer_weights(l, w_in, b_f, w_a2, b_a, gla_norm_g, w_pa, w_pb, w_o, ln1_g, ln1_b,
                            w_gate, w_up, w_down, ln2_g, ln2_b)
        qa, ka, va, kf, vf, lf, gq, gk, gv, lg = _inproj(hp, lw)
        oa = _fox_prompt(qa, ka, va)
        ob, sfin = _gla_prompt(gq, gk, lg, gv)
        h2 = hp.reshape(BATCH * LP, D_MODEL)
        h2 = _merge(h2, oa.reshape(BATCH * LP, FOX_W), ob.reshape(BATCH * LP, GLA_V), lw, TM_TOK)
        h2 = _ffn(h2, lw, TM_TOK)
        hp = h2.reshape(BATCH, LP, D_MODEL)
        kp.append(kf[:, PADF:].reshape(BATCH, L_REAL, FOX_HEADS, FOX_DH))
        vp.append(vf[:, PADF:].reshape(BATCH, L_REAL, FOX_HEADS, FOX_DH))
        lfp.append(lf[:, PADF:])
        gp.append(sfin)
        r1, r2, lfs_full, lgs = _sample_inproj(hs, lw)
        q_s = r1[:, :FOX_W].reshape(DEC_BATCH, FOX_HEADS, FOX_DH)
        k_s = r1[:, FOX_W:2 * FOX_W].reshape(DEC_BATCH, FOX_HEADS, FOX_DH)
        v_s = r1[:, 2 * FOX_W:].reshape(DEC_BATCH, FOX_HEADS, FOX_DH)
        lf_s = lfs_full[:, :FOX_HEADS]
        qt = jnp.tile(jnp.swapaxes(q_s, 1, 2), (1, 1, LANES // FOX_HEADS))
        lfn = jnp.tile(lf_s, (1, LANES // FOX_HEADS))[:, None, :]
        oa_s = _fox_decode(l, page_table, qt, k_s, v_s, lfn, cache_k, cache_v, cache_lf_t)
        ob_s, s_new = _gla_decode(r2[:, :GLA_K].reshape(DEC_BATCH, GLA_HEADS, GLA_DK),
                                  r2[:, GLA_K:2 * GLA_K].reshape(DEC_BATCH, GLA_HEADS, GLA_DK),
                                  lgs.reshape(DEC_BATCH, GLA_HEADS, GLA_DK),
                                  r2[:, 2 * GLA_K:].reshape(DEC_BATCH, GLA_HEADS, GLA_DV),
                                  state_gla[l])
        hs = _merge(hs, oa_s.reshape(DEC_BATCH, FOX_W).astype(BF16),
                    ob_s.reshape(DEC_BATCH, GLA_V), lw, DEC_BATCH)
        hs = _ffn(hs, lw, DEC_BATCH)
        ksr.append(k_s[:, None])
        vsr.append(v_s[:, None])
        lfs.append(lf_s[:, None])
        gs.append(s_new)

    y_prompt = hp[:, PADF + N_META:]
    y_sample = hs[:, None, :]
    return (y_prompt, y_sample, jnp.stack(kp), jnp.stack(vp), jnp.stack(lfp), jnp.stack(gp),
            jnp.stack(ksr), jnp.stack(vsr), jnp.stack(lfs), jnp.stack(gs))
```

```python
import functools

import numpy as np
import jax
import jax.numpy as jnp
from jax import lax
from jax.experimental import pallas as pl
from jax.experimental.pallas import tpu as pltpu

D_MODEL = 1024
BATCH = 2
SEQ = 8192
DEPTH = 2
DEC_BATCH = 32
PAST_LEN = 8192
PAGE_SIZE = 128
N_META = 16
FOX_HEADS = 8
FOX_DH = 64
FOX_W = FOX_HEADS * FOX_DH
GLA_HEADS = 4
GLA_DK = 64
GLA_DV = 128
GLA_K = GLA_HEADS * GLA_DK
GLA_V = GLA_HEADS * GLA_DV
GLA_RANK = 16
GLA_TAU = 16.0
D_FF = 2816
LN_EPS = 1e-5
NORM_EPS = 1e-6
NEG_INF = -1e30
ALPHA = (2.0 * DEPTH) ** 0.25
_SPLITS = (FOX_W, FOX_W, FOX_W, FOX_HEADS, GLA_K, GLA_K, GLA_V, GLA_V, GLA_RANK, D_MODEL, D_MODEL)
_OFF = np.concatenate([[0], np.cumsum(_SPLITS)]).tolist()

LANES = 128
SUBLANES = 8
VMEM_LIMIT = 56 * 1024 * 1024

L_REAL = SEQ + N_META
FOX_BLK = 256
LP = -(-L_REAL // FOX_BLK) * FOX_BLK
PADF = LP - L_REAL
GLA_CHUNK = 128
GLA_TILE = 256
TM_IN = 384
CUM_BLK = 128
TM_TOK = 512
N_PAGES = PAST_LEN // PAGE_SIZE
PAGES_PER_STEP = 8

F32 = jnp.float32
BF16 = jnp.bfloat16


def _dot(a, b):
    return jnp.dot(a, b, preferred_element_type=F32)


def _dot_nt(a, b):
    return lax.dot_general(a, b, (((1,), (1,)), ((), ())), preferred_element_type=F32)


def _dot_tn(a, b):
    return lax.dot_general(a, b, (((0,), (0,)), ((), ())), preferred_element_type=F32)


def _log_sigmoid(x):
    return jnp.minimum(x, 0.0) - jnp.log(1.0 + jnp.exp(-jnp.abs(x)))


def _sigmoid(x):
    return 1.0 / (1.0 + jnp.exp(-x))


def _layer_norm(y, g, b):
    mu = jnp.mean(y, axis=-1, keepdims=True)
    d = y - mu
    var = jnp.mean(d * d, axis=-1, keepdims=True)
    return d * lax.rsqrt(var + LN_EPS) * g + b


def _split3(x):
    hi = x.astype(BF16)
    r = x - hi.astype(F32)
    mid = r.astype(BF16)
    lo = (r - mid.astype(F32)).astype(BF16)
    return hi, mid, lo


def _inproj_kernel(x_ref, wqkv_ref, wg_ref, ws_ref, w2_ref, bs_ref, ba_ref, tri_ref,
                   q_ref, k_ref, v_ref, kf_ref, vf_ref, lf_ref, gq_ref, gk_ref, gv_ref, lg_ref,
                   carry_ref, *, tm):
    i = pl.program_id(1)

    @pl.when(i == 0)
    def _():
        carry_ref[...] = jnp.zeros_like(carry_ref)

    row = i * tm + lax.broadcasted_iota(jnp.int32, (tm, 1), 0)
    real = row >= PADF
    xb = jnp.where(real, x_ref[0], 0.0).astype(BF16)

    rs = _dot(xb, ws_ref[...]) + bs_ref[...]
    lf_full = jnp.where(real, _log_sigmoid(rs), 0.0)
    lf_ref[0] = lf_full[:, :FOX_HEADS]
    z = _dot(rs.astype(BF16), w2_ref[...]) + ba_ref[...]
    lg_ref[0] = _log_sigmoid(z) * (1.0 / GLA_TAU)

    carry = carry_ref[...]
    tri = tri_ref[...]
    cs = []
    for sb in range(tm // CUM_BLK):
        hi, mid, lo = _split3(lf_full[sb * CUM_BLK:(sb + 1) * CUM_BLK])
        c = _dot(tri, hi) + _dot(tri, mid) + _dot(tri, lo) + carry
        carry = c[CUM_BLK - 1:CUM_BLK]
        cs.append(c)
    carry_ref[...] = carry
    c = jnp.concatenate(cs, axis=0)

    r = _dot(xb, wqkv_ref[...])
    kf_ref[0] = r[:, FOX_W:2 * FOX_W]
    vf_ref[0] = r[:, 2 * FOX_W:3 * FOX_W]

    lane = lax.broadcasted_iota(jnp.int32, (1, LANES), 1)
    for h in range(FOX_HEADS):
        p, e = divmod(h, 2)
        dmask = (lane < FOX_DH) if e == 0 else (lane >= FOX_DH)
        xo = FOX_DH if e == 0 else 0
        ch = jnp.broadcast_to(c[:, h:h + 1], (tm, LANES))
        hi = ch.astype(BF16).astype(F32)
        r1 = ch - hi
        mid = r1.astype(BF16).astype(F32)
        lo = r1 - mid
        one3 = (lane >= xo + 3) & (lane < xo + 6)
        eq = jnp.where(lane == xo, hi, jnp.where(lane == xo + 1, mid, jnp.where(
            lane == xo + 2, lo, jnp.where(one3, 1.0, 0.0))))
        first3 = (lane >= xo) & (lane < xo + 3)
        ek = jnp.where(first3, 1.0, jnp.where(lane == xo + 3, jnp.where(real, -hi, NEG_INF), jnp.where(
            lane == xo + 4, -mid, jnp.where(lane == xo + 5, -lo, 0.0))))
        ev = jnp.where(lane == xo, 1.0, 0.0)
        rq = r[:, p * LANES:(p + 1) * LANES]
        rk = r[:, FOX_W + p * LANES:FOX_W + (p + 1) * LANES]
        rv = r[:, 2 * FOX_W + p * LANES:2 * FOX_W + (p + 1) * LANES]
        q_ref[0, h] = jnp.where(dmask, rq, eq).astype(BF16)
        k_ref[0, h] = jnp.where(dmask, rk, ek).astype(BF16)
        v_ref[0, h] = jnp.where(dmask, rv, ev).astype(BF16)

    rg = _dot(xb, wg_ref[...])
    gq_ref[0] = rg[:, :GLA_K]
    gk_ref[0] = rg[:, GLA_K:2 * GLA_K]
    gv_ref[0] = rg[:, 2 * GLA_K:].astype(BF16)


def _const_spec(shape):
    nd = len(shape)
    return pl.BlockSpec(shape, lambda *_: (0,) * nd, pipeline_mode=pl.Buffered(1))


def _inproj(hp, lw):
    tm = TM_IN
    nt = LP // tm
    tri = jnp.asarray(np.tril(np.ones((CUM_BLK, CUM_BLK), np.float32)), BF16)
    row3 = lambda w: pl.BlockSpec((1, tm, w), lambda b, i: (b, i, 0))
    head4 = pl.BlockSpec((1, FOX_HEADS, tm, LANES), lambda b, i: (b, 0, i, 0))
    out_shape = (
        jax.ShapeDtypeStruct((BATCH, FOX_HEADS, LP, LANES), BF16),
        jax.ShapeDtypeStruct((BATCH, FOX_HEADS, LP, LANES), BF16),
        jax.ShapeDtypeStruct((BATCH, FOX_HEADS, LP, LANES), BF16),
        jax.ShapeDtypeStruct((BATCH, LP, FOX_W), F32),
        jax.ShapeDtypeStruct((BATCH, LP, FOX_W), F32),
        jax.ShapeDtypeStruct((BATCH, LP, FOX_HEADS), F32),
        jax.ShapeDtypeStruct((BATCH, LP, GLA_K), F32),
        jax.ShapeDtypeStruct((BATCH, LP, GLA_K), F32),
        jax.ShapeDtypeStruct((BATCH, LP, GLA_V), BF16),
        jax.ShapeDtypeStruct((BATCH, LP, GLA_K), F32),
    )
    return pl.pallas_call(
        functools.partial(_inproj_kernel, tm=tm),
        grid=(BATCH, nt),
        in_specs=[row3(D_MODEL), _const_spec(lw["wqkv"].shape), _const_spec(lw["wgla"].shape),
                  _const_spec(lw["ws"].shape), _const_spec(lw["w2"].shape),
                  _const_spec(lw["bs"].shape), _const_spec(lw["ba"].shape), _const_spec(tri.shape)],
        out_specs=(head4, head4, head4, row3(FOX_W), row3(FOX_W), row3(FOX_HEADS),
                   row3(GLA_K), row3(GLA_K), row3(GLA_V), row3(GLA_K)),
        out_shape=out_shape,
        scratch_shapes=[pltpu.VMEM((1, LANES), F32)],
        compiler_params=pltpu.CompilerParams(
            dimension_semantics=("parallel", "arbitrary"), vmem_limit_bytes=VMEM_LIMIT),
        name="prompt_inproj",
    )(hp, lw["wqkv"], lw["wgla"], lw["ws"], lw["w2"], lw["bs"], lw["ba"], tri)


def _fox_kernel(q_ref, k_ref, v_ref, o_ref, vt_sc, acc_sc, m_sc, s0_sc, s1_sc):
    i = pl.program_id(2)
    blk = FOX_BLK
    half = blk // 2

    @pl.when(i == 0)
    def _():
        for e in range(2):
            def body(c, carry):
                for s in range(2):
                    t = v_ref[0, e, pl.ds(c * blk + s * half, half), :].astype(F32)
                    vt_sc[e, c, :, s * half:(s + 1) * half] = t.T.astype(BF16)
                return carry
            lax.fori_loop(0, LP // blk, body, 0)

    for e in range(2):
        m_sc[e] = jnp.full((1, blk), NEG_INF, F32)
        acc_sc[e] = jnp.zeros((LANES, blk), F32)

    sbuf = (s0_sc, s1_sc)
    causal = (lax.broadcasted_iota(jnp.int32, (blk, 1), 0)
              <= lax.broadcasted_iota(jnp.int32, (1, blk), 1))

    def scores(j, dst):
        start = j * blk if isinstance(j, int) else pl.multiple_of(j * blk, blk)
        for e in range(2):
            dst[e] = _dot_nt(k_ref[0, e, pl.ds(start, blk), :], q_ref[0, e])

    def step(j, cur, masked, prefetch):
        if prefetch:
            scores(j + 1, sbuf[1 - cur])
        for e in range(2):
            st = sbuf[cur][e]
            if masked:
                st = jnp.where(causal, st, NEG_INF)
            m_old = m_sc[e]
            m_new = jnp.maximum(m_old, jnp.max(st, axis=0, keepdims=True))
            alpha = jnp.exp(m_old - m_new)
            pt = jnp.exp(st - m_new).astype(BF16)
            acc_sc[e] = alpha * acc_sc[e] + _dot(vt_sc[e, j], pt)
            m_sc[e] = m_new

    scores(0, s0_sc)

    def pair(jj, carry):
        step(2 * jj, 0, False, True)
        step(2 * jj + 1, 1, False, True)
        return carry
    lax.fori_loop(0, i // 2, pair, 0)
    odd = (i % 2) == 1

    @pl.when(odd)
    def _():
        step(i - 1, 0, False, True)
        step(i, 1, True, False)

    @pl.when(jnp.logical_not(odd))
    def _():
        step(i, 0, True, False)

    a0 = acc_sc[0]
    a1 = acc_sc[1]
    row = lax.broadcasted_iota(jnp.int32, (LANES, 1), 0)
    ot = jnp.where(row < FOX_DH, a0 / a0[FOX_DH:FOX_DH + 1, :], a1 / a1[0:1, :])
    o_ref[0] = ot.T.astype(BF16)


def _fox_prompt(q, k, v):
    nq = LP // FOX_BLK
    return pl.pallas_call(
        _fox_kernel,
        grid=(BATCH, FOX_HEADS // 2, nq),
        in_specs=[pl.BlockSpec((1, 2, FOX_BLK, LANES), lambda b, p, i: (b, p, i, 0)),
                  pl.BlockSpec((1, 2, LP, LANES), lambda b, p, i: (b, p, 0, 0)),
                  pl.BlockSpec((1, 2, LP, LANES), lambda b, p, i: (b, p, 0, 0))],
        out_specs=pl.BlockSpec((1, FOX_BLK, LANES), lambda b, p, i: (b, i, p)),
        out_shape=jax.ShapeDtypeStruct((BATCH, LP, FOX_W), BF16),
        scratch_shapes=[pltpu.VMEM((2, nq, LANES, FOX_BLK), BF16),
                        pltpu.VMEM((2, LANES, FOX_BLK), F32),
                        pltpu.VMEM((2, 1, FOX_BLK), F32),
                        pltpu.VMEM((2, FOX_BLK, FOX_BLK), F32),
                        pltpu.VMEM((2, FOX_BLK, FOX_BLK), F32)],
        compiler_params=pltpu.CompilerParams(
            dimension_semantics=("parallel", "parallel", "arbitrary"), vmem_limit_bytes=VMEM_LIMIT),
        name="fox_prompt",
    )(q, k, v)


def _gla_tables():
    c = GLA_CHUNK
    t = np.arange(c)[:, None]
    j = np.arange(c)[None, :]
    mats = [(j <= t).astype(np.float32), (j > t).astype(np.float32)]
    masks = [np.eye(c, dtype=np.float32)]
    blk = c
    while blk >= 2:
        half = blk // 2
        mid = (t // blk) * blk + half
        mats.append((j <= t).astype(np.float32) - (j <= mid).astype(np.float32))
        s = j
        masks.append((((t // blk) == (s // blk)) & ((t % blk) >= half) & ((s % blk) < half))
                     .astype(np.float32))
        blk = half
    return np.concatenate(mats, axis=0), np.stack(masks, axis=0)


_GLA_LEVELS = 7


def _gla_kernel(q_ref, k_ref, lg_ref, v_ref, dall_ref, masks_ref, o_ref, sfin_ref, st_sc):
    i = pl.program_id(2)
    c = GLA_CHUNK

    @pl.when(i == 0)
    def _():
        st_sc[...] = jnp.zeros_like(st_sc)

    lane = lax.broadcasted_iota(jnp.int32, (1, LANES), 1)
    hmask = (lane < GLA_DK, lane >= GLA_DK)

    for ci in range(GLA_TILE // c):
        rows = slice(ci * c, (ci + 1) * c)
        q = q_ref[0, rows, :]
        k = k_ref[0, rows, :]
        lg = lg_ref[0, rows, :]
        hi = lg.astype(BF16)
        lo = (lg - hi.astype(F32)).astype(BF16)
        e2 = _dot(dall_ref[...], jnp.concatenate([hi, lo], axis=1))
        ex = e2[:, :LANES] + e2[:, LANES:]
        bc = ex[0:c]
        q_in = q * jnp.exp(bc)
        k_dec = k * jnp.exp(ex[c:2 * c])
        qs = [q]
        ks = [k.astype(BF16)]
        for lv in range(_GLA_LEVELS):
            f = jnp.exp(-jnp.abs(ex[(2 + lv) * c:(3 + lv) * c]))
            qs.append(q * f)
            ks.append((k * f).astype(BF16))
        decay_all = jnp.exp(bc[c - 1:c])
        for h in range(2):
            a = jnp.zeros((c, c), F32)
            for lv in range(_GLA_LEVELS + 1):
                ql = jnp.where(hmask[h], qs[lv], 0.0).astype(BF16)
                a = a + masks_ref[lv] * _dot_nt(ql, ks[lv])
            vh = v_ref[0, rows, h * GLA_DV:(h + 1) * GLA_DV]
            st = st_sc[h]
            o = _dot(a.astype(BF16), vh) + _dot_nt(jnp.where(hmask[h], q_in, 0.0).astype(BF16),
                                                   st.astype(BF16))
            o_ref[0, rows, h * GLA_DV:(h + 1) * GLA_DV] = o
            kd = jnp.where(hmask[h], k_dec, 0.0).astype(BF16)
            st_sc[h] = decay_all * st + _dot_tn(vh, kd)

    @pl.when(i == pl.num_programs(2) - 1)
    def _():
        for h in range(2):
            s = st_sc[h].T
            sfin_ref[0, h] = s[h * GLA_DK:(h + 1) * GLA_DK, :]


def _gla_prompt(gq, gk, lg, gv):
    dall_np, masks_np = _gla_tables()
    dall = jnp.asarray(dall_np, BF16)
    masks = jnp.asarray(masks_np, F32)
    t = GLA_TILE
    pair = pl.BlockSpec((1, t, LANES), lambda b, p, i: (b, i, p))
    wide = pl.BlockSpec((1, t, 2 * GLA_DV), lambda b, p, i: (b, i, p))
    return pl.pallas_call(
        _gla_kernel,
        grid=(BATCH, GLA_HEADS // 2, LP // t),
        in_specs=[pair, pair, pair, wide, _const_spec(dall.shape), _const_spec(masks.shape)],
        out_specs=(wide, pl.BlockSpec((1, 2, GLA_DK, GLA_DV), lambda b, p, i: (b, p, 0, 0))),
        out_shape=(jax.ShapeDtypeStruct((BATCH, LP, GLA_V), F32),
                   jax.ShapeDtypeStruct((BATCH, GLA_HEADS, GLA_DK, GLA_DV), F32)),
        scratch_shapes=[pltpu.VMEM((2, GLA_DV, LANES), F32)],
        compiler_params=pltpu.CompilerParams(
            dimension_semantics=("parallel", "parallel", "arbitrary"), vmem_limit_bytes=VMEM_LIMIT),
        name="gla_prompt",
    )(gq, gk, lg, gv, dall, masks)


def _merge_kernel(h_ref, oa_ref, ob_ref, wg_ref, wpa_ref, wpb_ref, wo_ref, gn_ref, g_ref, b_ref,
                  out_ref):
    x = h_ref[...]
    xb = x.astype(BF16)
    r = _dot(xb, wg_ref[...])
    rb = r[:, :GLA_V]
    ga = r[:, GLA_V:GLA_V + D_MODEL]
    gb = r[:, GLA_V + D_MODEL:]
    ob = ob_ref[...]
    parts = []
    for hd in range(GLA_HEADS):
        o = ob[:, hd * GLA_DV:(hd + 1) * GLA_DV]
        ms = jnp.mean(o * o, axis=-1, keepdims=True)
        parts.append(o * lax.rsqrt(ms + NORM_EPS) * gn_ref[...])
    obn = jnp.concatenate(parts, axis=1) * (rb * _sigmoid(rb))
    y_a = _dot(oa_ref[...], wpa_ref[...])
    y_b = _dot(obn.astype(BF16), wpb_ref[...])
    mixed = _sigmoid(ga) * y_a + _sigmoid(gb) * y_b
    y = ALPHA * x + _dot(mixed.astype(BF16), wo_ref[...])
    out_ref[...] = _layer_norm(y, g_ref[...], b_ref[...])


def _merge(h, oa, ob, lw, tm):
    m = h.shape[0]
    row = lambda w: pl.BlockSpec((tm, w), lambda i: (i, 0))
    ws = (lw["wmg"], lw["wpa"], lw["wpb"], lw["wo"], lw["gn"], lw["ln1g"], lw["ln1b"])
    return pl.pallas_call(
        _merge_kernel,
        grid=(m // tm,),
        in_specs=[row(D_MODEL), row(FOX_W), row(GLA_V)] + [_const_spec(w.shape) for w in ws],
        out_specs=row(D_MODEL),
        out_shape=jax.ShapeDtypeStruct((m, D_MODEL), F32),
        compiler_params=pltpu.CompilerParams(
            dimension_semantics=("parallel",), vmem_limit_bytes=VMEM_LIMIT),
        name="merge",
    )(h, oa, ob, *ws)


def _ffn_kernel(h_ref, wgate_ref, wup_ref, wdown_ref, g_ref, b_ref, out_ref):
    x = h_ref[...]
    xb = x.astype(BF16)
    gt = _dot(xb, wgate_ref[...])
    up = _dot(xb, wup_ref[...])
    hdn = (gt * _sigmoid(gt) * up).astype(BF16)
    y = ALPHA * x + _dot(hdn, wdown_ref[...])
    out_ref[...] = _layer_norm(y, g_ref[...], b_ref[...])


def _ffn(h, lw, tm):
    m = h.shape[0]
    row = pl.BlockSpec((tm, D_MODEL), lambda i: (i, 0))
    ws = (lw["wgate"], lw["wup"], lw["wdown"], lw["ln2g"], lw["ln2b"])
    return pl.pallas_call(
        _ffn_kernel,
        grid=(m // tm,),
        in_specs=[row] + [_const_spec(w.shape) for w in ws],
        out_specs=row,
        out_shape=jax.ShapeDtypeStruct((m, D_MODEL), F32),
        compiler_params=pltpu.CompilerParams(
            dimension_semantics=("parallel",), vmem_limit_bytes=VMEM_LIMIT),
        name="ffn",
    )(h, *ws)


def _sample_inproj_kernel(x_ref, wqkv_ref, wg_ref, ws_ref, w2_ref, bs_ref, ba_ref,
                          r1_ref, r2_ref, lf_ref, lg_ref):
    xb = x_ref[...].astype(BF16)
    r1_ref[...] = _dot(xb, wqkv_ref[...])
    r2_ref[...] = _dot(xb, wg_ref[...])
    rs = _dot(xb, ws_ref[...]) + bs_ref[...]
    lf_ref[...] = _log_sigmoid(rs)
    z = _dot(rs.astype(BF16), w2_ref[...]) + ba_ref[...]
    lg_ref[...] = _log_sigmoid(z) * (1.0 / GLA_TAU)


def _sample_inproj(x, lw):
    ws = (lw["wqkv"], lw["wgla"], lw["ws"], lw["w2"], lw["bs"], lw["ba"])
    full = lambda shape: pl.BlockSpec(shape, lambda i: (0,) * len(shape))
    out_shape = (jax.ShapeDtypeStruct((DEC_BATCH, 3 * FOX_W), F32),
                 jax.ShapeDtypeStruct((DEC_BATCH, 2 * GLA_K + GLA_V), F32),
                 jax.ShapeDtypeStruct((DEC_BATCH, LANES), F32),
                 jax.ShapeDtypeStruct((DEC_BATCH, GLA_K), F32))
    return pl.pallas_call(
        _sample_inproj_kernel,
        grid=(1,),
        in_specs=[full(x.shape)] + [full(w.shape) for w in ws],
        out_specs=tuple(full(s.shape) for s in out_shape),
        out_shape=out_shape,
        compiler_params=pltpu.CompilerParams(vmem_limit_bytes=VMEM_LIMIT),
        name="sample_inproj",
    )(x, *ws)


def _decode_tables():
    j = np.arange(LANES)
    ut = np.concatenate([j[:, None] > j[None, :], np.ones((LANES, LANES), bool)], axis=1)
    nr = PAGES_PER_STEP * FOX_HEADS
    r = np.arange(nr)
    same = (r[:, None] % FOX_HEADS) == (r[None, :] % FOX_HEADS)
    us = np.zeros((LANES, nr), bool)
    us[:nr] = same & (r[None, :] > r[:, None])
    us[nr:nr + FOX_HEADS] = (r[None, :] % FOX_HEADS) == np.arange(FOX_HEADS)[:, None]
    return ut.astype(np.float32), us.astype(np.float32)


def _fox_decode_kernel(pt_ref, qb_ref, knb_ref, vnb_ref, lfn_ref, ut_ref, us_ref, *rest):
    npg = PAGES_PER_STEP
    kp = rest[0:npg]
    vp = rest[npg:2 * npg]
    lp = rest[2 * npg:3 * npg]
    o_ref = rest[3 * npg]
    m_sc, l_sc, acc_sc, carry_sc = rest[3 * npg + 1:]
    del pt_ref
    j = pl.program_id(1)
    nh = FOX_HEADS
    nr = npg * nh

    @pl.when(j == 0)
    def _():
        rows = [jnp.sum(qb_ref[0, h] * knb_ref[0, h], axis=0, keepdims=True) for h in range(nh)]
        m_sc[...] = jnp.concatenate(rows, axis=0)
        lane = lax.broadcasted_iota(jnp.int32, (nh, LANES), 1)
        l_sc[...] = jnp.where(lane == 0, 1.0, 0.0)
        lane_d = lax.broadcasted_iota(jnp.int32, (FOX_DH, LANES), 1)
        for h in range(nh):
            acc_sc[h] = jnp.where(lane_d == 0, vnb_ref[0, h], 0.0)
        carry_sc[...] = lfn_ref[0]

    srows = [[None] * nh for _ in range(npg)]
    for h in range(nh):
        qh = qb_ref[0, h]
        for g in range(npg):
            srows[g][h] = jnp.sum(qh * kp[g][h], axis=0, keepdims=True)
    s = [jnp.concatenate(srows[g], axis=0) for g in range(npg)]

    lfc = jnp.concatenate([lp[g][...] for g in range(npg)], axis=0)
    w = _dot(jnp.concatenate(_split3(lfc), axis=0), ut_ref[...])
    wsum = w[0:nr] + w[nr:2 * nr] + w[2 * nr:3 * nr]
    x = _dot(us_ref[...], jnp.concatenate(_split3(wsum[:, LANES:]), axis=1))
    xs = x[:, :LANES] + x[:, LANES:2 * LANES] + x[:, 2 * LANES:]
    carry = carry_sc[...]
    sb = [s[g] + (wsum[g * nh:(g + 1) * nh, :LANES] + xs[g * nh:(g + 1) * nh] + carry)
          for g in range(npg)]
    carry_sc[...] = carry + xs[nr:nr + nh]

    mx = sb[0]
    for g in range(1, npg):
        mx = jnp.maximum(mx, sb[g])
    m_old = m_sc[...]
    m_new = jnp.maximum(m_old, jnp.max(mx, axis=1, keepdims=True))
    alpha = jnp.exp(m_old - m_new)
    p = [jnp.exp(sb[g] - m_new) for g in range(npg)]
    psum = p[0]
    for g in range(1, npg):
        psum = psum + p[g]
    l_sc[...] = alpha * l_sc[...] + psum
    m_sc[...] = m_new
    for h in range(nh):
        a = acc_sc[h] * alpha[h:h + 1, :]
        for g in range(npg):
            a = a + p[g][h:h + 1, :] * vp[g][h]
        acc_sc[h] = a

    @pl.when(j == pl.num_programs(1) - 1)
    def _():
        ltot = jnp.broadcast_to(jnp.sum(l_sc[...], axis=1, keepdims=True), (nh, LANES))
        accn = jnp.concatenate([acc_sc[h] / ltot[h:h + 1, :] for h in range(nh)], axis=0)
        ones = jnp.ones((SUBLANES, LANES), BF16)
        hi, mid, lo = _split3(accn)
        o8 = _dot_nt(ones, hi) + _dot_nt(ones, mid) + _dot_nt(ones, lo)
        o_ref[0] = o8[0:1]


def _fox_decode(layer, page_table, qb, knb, vnb, lfn, cache_kt, cache_vt, cache_lft):
    npg = PAGES_PER_STEP
    nsteps = N_PAGES // npg
    ut_np, us_np = _decode_tables()
    ut = jnp.asarray(ut_np, BF16)
    us = jnp.asarray(us_np, BF16)

    def page_map(g, tail):
        def f(b, j, pt):
            return (layer, pt[b, (nsteps - 1 - j) * npg + g]) + tail
        return f

    kv_specs = [pl.BlockSpec((None, None, FOX_HEADS, FOX_DH, PAGE_SIZE), page_map(g, (0, 0, 0)))
                for g in range(npg)]
    lf_specs = [pl.BlockSpec((None, None, FOX_HEADS, PAGE_SIZE), page_map(g, (0, 0)))
                for g in range(npg)]
    per_b = lambda shape: pl.BlockSpec((1,) + shape, lambda b, j, pt: (b,) + (0,) * len(shape))
    const = lambda shape: pl.BlockSpec(shape, lambda b, j, pt: (0,) * len(shape))
    hdl = (FOX_HEADS, FOX_DH, LANES)
    grid_spec = pltpu.PrefetchScalarGridSpec(
        num_scalar_prefetch=1,
        grid=(DEC_BATCH, nsteps),
        in_specs=[per_b(hdl), per_b(hdl), per_b(hdl), per_b((FOX_HEADS, LANES)),
                  const(ut.shape), const(us.shape)] + kv_specs + kv_specs + lf_specs,
        out_specs=per_b((1, FOX_W)),
        scratch_shapes=[pltpu.VMEM((FOX_HEADS, LANES), F32), pltpu.VMEM((FOX_HEADS, LANES), F32),
                        pltpu.VMEM(hdl, F32), pltpu.VMEM((FOX_HEADS, LANES), F32)],
    )
    return pl.pallas_call(
        _fox_decode_kernel,
        grid_spec=grid_spec,
        out_shape=jax.ShapeDtypeStruct((DEC_BATCH, 1, FOX_W), F32),
        compiler_params=pltpu.CompilerParams(
            dimension_semantics=("parallel", "arbitrary"), vmem_limit_bytes=VMEM_LIMIT),
        name="fox_decode",
    )(page_table, qb, knb, vnb, lfn, ut, us,
      *([cache_kt] * npg), *([cache_vt] * npg), *([cache_lft] * npg))


def _gla_decode_kernel(q_ref, k_ref, g_ref, v_ref, s_ref, o_ref, sn_ref):
    eye = (lax.broadcasted_iota(jnp.int32, (GLA_DK, GLA_DK), 0)
           == lax.broadcasted_iota(jnp.int32, (GLA_DK, GLA_DK), 1))

    def col(r):
        return jnp.sum(jnp.where(eye, jnp.broadcast_to(r, (GLA_DK, GLA_DK)), 0.0),
                       axis=1, keepdims=True)

    for h in range(GLA_HEADS):
        qc = col(q_ref[0, h:h + 1, :])
        kc = col(k_ref[0, h:h + 1, :])
        ac = col(jnp.exp(g_ref[0, h:h + 1, :]))
        sn = ac * s_ref[0, h] + kc * v_ref[0, h:h + 1, :]
        sn_ref[0, h] = sn
        o_ref[0, h:h + 1, :] = jnp.sum(qc * sn, axis=0, keepdims=True)


def _gla_decode(gq, gk, lg, gv, state):
    hk = pl.BlockSpec((1, GLA_HEADS, GLA_DK), lambda b: (b, 0, 0))
    hv = pl.BlockSpec((1, GLA_HEADS, GLA_DV), lambda b: (b, 0, 0))
    st = pl.BlockSpec((1, GLA_HEADS, GLA_DK, GLA_DV), lambda b: (b, 0, 0, 0))
    return pl.pallas_call(
        _gla_decode_kernel,
        grid=(DEC_BATCH,),
        in_specs=[hk, hk, hk, hv, st],
        out_specs=(hv, st),
        out_shape=(jax.ShapeDtypeStruct((DEC_BATCH, GLA_HEADS, GLA_DV), F32),
                   jax.ShapeDtypeStruct((DEC_BATCH, GLA_HEADS, GLA_DK, GLA_DV), F32)),
        compiler_params=pltpu.CompilerParams(
            dimension_semantics=("parallel",), vmem_limit_bytes=VMEM_LIMIT),
        name="gla_decode",
    )(gq, gk, lg, gv, state)


def _layer_weights(l, w_in, b_f, w_a2, b_a, gla_norm_g, w_pa, w_pb, w_o, ln1_g, ln1_b,
                   w_gate, w_up, w_down, ln2_g, ln2_b):
    w = w_in[l]
    o = _OFF
    sc_f = FOX_DH ** -0.5
    sc_g = GLA_DK ** -0.5
    seg = lambda a: w[:, o[a]:o[a + 1]]
    wqkv = jnp.concatenate([seg(0) * sc_f, seg(1), seg(2)], axis=1).astype(BF16)
    wgla = jnp.concatenate([seg(4) * sc_g, seg(5), seg(6)], axis=1).astype(BF16)
    ws = jnp.zeros((D_MODEL, LANES), F32)
    ws = ws.at[:, :FOX_HEADS].set(seg(3)).at[:, FOX_HEADS:FOX_HEADS + GLA_RANK].set(seg(8))
    w2 = jnp.zeros((LANES, GLA_K), F32).at[FOX_HEADS:FOX_HEADS + GLA_RANK].set(w_a2[l])
    bs = jnp.zeros((1, LANES), F32).at[0, :FOX_HEADS].set(b_f[l])
    wmg = jnp.concatenate([seg(7), seg(9), seg(10)], axis=1).astype(BF16)
    return dict(
        wqkv=wqkv, wgla=wgla, ws=ws.astype(BF16), w2=w2.astype(BF16), bs=bs, ba=b_a[l][None],
        wmg=wmg, wpa=w_pa[l].astype(BF16), wpb=w_pb[l].astype(BF16), wo=w_o[l].astype(BF16),
        gn=gla_norm_g[l][None], ln1g=ln1_g[l][None], ln1b=ln1_b[l][None],
        wgate=w_gate[l].astype(BF16), wup=w_up[l].astype(BF16), wdown=w_down[l].astype(BF16),
        ln2g=ln2_g[l][None], ln2b=ln2_b[l][None])


def kernel(x_prompt, x_sample, cache_k, cache_v, cache_lf, state_gla, page_table, meta, w_in, b_f,
           w_a2, b_a, gla_norm_g, w_pa, w_pb, w_o, ln1_g, ln1_b, w_gate, w_up, w_down, ln2_g, ln2_b):
    assert x_prompt.shape == (BATCH, SEQ, D_MODEL) and x_sample.shape == (DEC_BATCH, 1, D_MODEL)
    front = jnp.concatenate([jnp.zeros((PADF, D_MODEL), F32), meta.astype(F32)], axis=0)
    hp = jnp.concatenate([jnp.broadcast_to(front[None], (BATCH, PADF + N_META, D_MODEL)), x_prompt],
                         axis=1)
    hs = x_sample.reshape(DEC_BATCH, D_MODEL)
    cache_kt = jnp.transpose(cache_k, (0, 1, 3, 4, 2))
    cache_vt = jnp.transpose(cache_v, (0, 1, 3, 4, 2))
    cache_lft = jnp.transpose(cache_lf, (0, 1, 3, 2))

    kp, vp, lfp, gp, ksr, vsr, lfs, gs = [], [], [], [], [], [], [], []
    for l in range(DEPTH):
        lw = _layer_weights(l, w_in, b_f, w_a2, b_a, gla_norm_g, w_pa, w_pb, w_o, ln1_g, ln1_b,
                            w_gate, w_up, w_down, ln2_g, ln2_b)
        qa, ka, va, kf, vf, lf, gq, gk, gv, lg = _inproj(hp, lw)
        oa = _fox_prompt(qa, ka, va)
        ob, sfin = _gla_prompt(gq, gk, lg, gv)
        h2 = hp.reshape(BATCH * LP, D_MODEL)
        h2 = _merge(h2, oa.reshape(BATCH * LP, FOX_W), ob.reshape(BATCH * LP, GLA_V), lw, TM_TOK)
        h2 = _ffn(h2, lw, TM_TOK)
        hp = h2.reshape(BATCH, LP, D_MODEL)
        kp.append(kf[:, PADF:].reshape(BATCH, L_REAL, FOX_HEADS, FOX_DH))
        vp.append(vf[:, PADF:].reshape(BATCH, L_REAL, FOX_HEADS, FOX_DH))
        lfp.append(lf[:, PADF:])
        gp.append(sfin)
        r1, r2, lfs_full, lgs = _sample_inproj(hs, lw)
        q_s = r1[:, :FOX_W].reshape(DEC_BATCH, FOX_HEADS, FOX_DH)
        k_s = r1[:, FOX_W:2 * FOX_W].reshape(DEC_BATCH, FOX_HEADS, FOX_DH)
        v_s = r1[:, 2 * FOX_W:].reshape(DEC_BATCH, FOX_HEADS, FOX_DH)
        lf_s = lfs_full[:, :FOX_HEADS]
        lanes = lambda t: jnp.broadcast_to(t[..., None], t.shape + (LANES,))
        oa_s = _fox_decode(l, page_table, lanes(q_s), lanes(k_s), lanes(v_s), lanes(lf_s),
                           cache_kt, cache_vt, cache_lft)
        ob_s, s_new = _gla_decode(r2[:, :GLA_K].reshape(DEC_BATCH, GLA_HEADS, GLA_DK),
                                  r2[:, GLA_K:2 * GLA_K].reshape(DEC_BATCH, GLA_HEADS, GLA_DK),
                                  lgs.reshape(DEC_BATCH, GLA_HEADS, GLA_DK),
                                  r2[:, 2 * GLA_K:].reshape(DEC_BATCH, GLA_HEADS, GLA_DV),
                                  state_gla[l])
        hs = _merge(hs, oa_s.reshape(DEC_BATCH, FOX_W).astype(BF16),
                    ob_s.reshape(DEC_BATCH, GLA_V), lw, DEC_BATCH)
        hs = _ffn(hs, lw, DEC_BATCH)
        ksr.append(k_s[:, None])
        vsr.append(v_s[:, None])
        lfs.append(lf_s[:, None])
        gs.append(s_new)

    y_prompt = hp[:, PADF + N_META:]
    y_sample = hs[:, None, :]
    return (y_prompt, y_sample, jnp.stack(kp), jnp.stack(vp), jnp.stack(lfp), jnp.stack(gp),
            jnp.stack(ksr), jnp.stack(vsr), jnp.stack(lfs), jnp.stack(gs))
```

```python
import functools

import numpy as np
import jax
import jax.numpy as jnp
from jax import lax
from jax.experimental import pallas as pl
from jax.experimental.pallas import tpu as pltpu

D_MODEL = 1024
BATCH = 2
SEQ = 8192
DEPTH = 2
DEC_BATCH = 32
PAST_LEN = 8192
PAGE_SIZE = 128
N_META = 16
FOX_HEADS = 8
FOX_DH = 64
FOX_W = FOX_HEADS * FOX_DH
GLA_HEADS = 4
GLA_DK = 64
GLA_DV = 128
GLA_K = GLA_HEADS * GLA_DK
GLA_V = GLA_HEADS * GLA_DV
GLA_RANK = 16
GLA_TAU = 16.0
D_FF = 2816
LN_EPS = 1e-5
NORM_EPS = 1e-6
NEG_INF = -1e30
ALPHA = (2.0 * DEPTH) ** 0.25
_SPLITS = (FOX_W, FOX_W, FOX_W, FOX_HEADS, GLA_K, GLA_K, GLA_V, GLA_V, GLA_RANK, D_MODEL, D_MODEL)
_OFF = np.concatenate([[0], np.cumsum(_SPLITS)]).tolist()

LANES = 128
SUBLANES = 8
VMEM_LIMIT = 56 * 1024 * 1024

L_REAL = SEQ + N_META
FOX_BLK = 256
FOX_BQ = 768
LOG2E = float(np.log2(np.e))
LP = -(-L_REAL // FOX_BLK) * FOX_BLK
PADF = LP - L_REAL
GLA_CHUNK = 128
GLA_TILE = 256
TM_IN = 384
CUM_BLK = 128
TM_TOK = 512
N_PAGES = PAST_LEN // PAGE_SIZE
PAGES_PER_STEP = 16

F32 = jnp.float32
BF16 = jnp.bfloat16


def _dot(a, b):
    return jnp.dot(a, b, preferred_element_type=F32)


def _dot_nt(a, b):
    return lax.dot_general(a, b, (((1,), (1,)), ((), ())), preferred_element_type=F32)


def _dot_tn(a, b):
    return lax.dot_general(a, b, (((0,), (0,)), ((), ())), preferred_element_type=F32)


def _log_sigmoid(x):
    return jnp.minimum(x, 0.0) - jnp.log(1.0 + jnp.exp(-jnp.abs(x)))


def _sigmoid(x):
    return 1.0 / (1.0 + jnp.exp(-x))


def _layer_norm(y, g, b):
    mu = jnp.mean(y, axis=-1, keepdims=True)
    d = y - mu
    var = jnp.mean(d * d, axis=-1, keepdims=True)
    return d * lax.rsqrt(var + LN_EPS) * g + b


def _split3(x):
    hi = x.astype(BF16)
    r = x - hi.astype(F32)
    mid = r.astype(BF16)
    lo = (r - mid.astype(F32)).astype(BF16)
    return hi, mid, lo


def _inproj_kernel(x_ref, wqkv_ref, wg_ref, ws_ref, w2_ref, bs_ref, ba_ref, tri_ref,
                   q_ref, k_ref, v_ref, kf_ref, vf_ref, lf_ref, gq_ref, gk_ref, gv_ref, lg_ref,
                   carry_ref, *, tm):
    i = pl.program_id(1)

    @pl.when(i == 0)
    def _():
        carry_ref[...] = jnp.zeros_like(carry_ref)

    row = i * tm + lax.broadcasted_iota(jnp.int32, (tm, 1), 0)
    real = row >= PADF
    xb = jnp.where(real, x_ref[0], 0.0).astype(BF16)

    rs = _dot(xb, ws_ref[...]) + bs_ref[...]
    lf_full = jnp.where(real, _log_sigmoid(rs), 0.0)
    lf_ref[0] = lf_full[:, :FOX_HEADS]
    z = _dot(rs.astype(BF16), w2_ref[...]) + ba_ref[...]
    lg_ref[0] = _log_sigmoid(z) * (1.0 / GLA_TAU)

    carry = carry_ref[...]
    tri = tri_ref[...]
    cs = []
    for sb in range(tm // CUM_BLK):
        hi, mid, lo = _split3(lf_full[sb * CUM_BLK:(sb + 1) * CUM_BLK])
        c = _dot(tri, hi) + _dot(tri, mid) + _dot(tri, lo) + carry
        carry = c[CUM_BLK - 1:CUM_BLK]
        cs.append(c)
    carry_ref[...] = carry
    c = jnp.concatenate(cs, axis=0) * LOG2E

    r = _dot(xb, wqkv_ref[...])
    kf_ref[0] = r[:, FOX_W:2 * FOX_W]
    vf_ref[0] = r[:, 2 * FOX_W:3 * FOX_W]

    lane = lax.broadcasted_iota(jnp.int32, (1, LANES), 1)
    for h in range(FOX_HEADS):
        p, e = divmod(h, 2)
        dmask = (lane < FOX_DH) if e == 0 else (lane >= FOX_DH)
        xo = FOX_DH if e == 0 else 0
        ch = jnp.broadcast_to(c[:, h:h + 1], (tm, LANES))
        hi = ch.astype(BF16).astype(F32)
        r1 = ch - hi
        mid = r1.astype(BF16).astype(F32)
        lo = r1 - mid
        one3 = (lane >= xo + 3) & (lane < xo + 6)
        eq = jnp.where(lane == xo, hi, jnp.where(lane == xo + 1, mid, jnp.where(
            lane == xo + 2, lo, jnp.where(one3, 1.0, 0.0))))
        first3 = (lane >= xo) & (lane < xo + 3)
        ek = jnp.where(first3, 1.0, jnp.where(lane == xo + 3, jnp.where(real, -hi, NEG_INF), jnp.where(
            lane == xo + 4, -mid, jnp.where(lane == xo + 5, -lo, 0.0))))
        ev = jnp.where(lane == xo, 1.0, 0.0)
        rq = r[:, p * LANES:(p + 1) * LANES]
        rk = r[:, FOX_W + p * LANES:FOX_W + (p + 1) * LANES]
        rv = r[:, 2 * FOX_W + p * LANES:2 * FOX_W + (p + 1) * LANES]
        q_ref[0, h] = jnp.where(dmask, rq * LOG2E, eq).astype(BF16)
        k_ref[0, h] = jnp.where(dmask, rk, ek).astype(BF16)
        v_ref[0, h] = jnp.where(dmask, rv, ev).astype(BF16)

    rg = _dot(xb, wg_ref[...])
    gq_ref[0] = rg[:, :GLA_K]
    gk_ref[0] = rg[:, GLA_K:2 * GLA_K]
    gv_ref[0] = rg[:, 2 * GLA_K:].astype(BF16)


def _const_spec(shape):
    nd = len(shape)
    return pl.BlockSpec(shape, lambda *_: (0,) * nd, pipeline_mode=pl.Buffered(1))


def _inproj(hp, lw):
    tm = TM_IN
    nt = LP // tm
    tri = jnp.asarray(np.tril(np.ones((CUM_BLK, CUM_BLK), np.float32)), BF16)
    row3 = lambda w: pl.BlockSpec((1, tm, w), lambda b, i: (b, i, 0))
    head4 = pl.BlockSpec((1, FOX_HEADS, tm, LANES), lambda b, i: (b, 0, i, 0))
    out_shape = (
        jax.ShapeDtypeStruct((BATCH, FOX_HEADS, LP, LANES), BF16),
        jax.ShapeDtypeStruct((BATCH, FOX_HEADS, LP, LANES), BF16),
        jax.ShapeDtypeStruct((BATCH, FOX_HEADS, LP, LANES), BF16),
        jax.ShapeDtypeStruct((BATCH, LP, FOX_W), F32),
        jax.ShapeDtypeStruct((BATCH, LP, FOX_W), F32),
        jax.ShapeDtypeStruct((BATCH, LP, FOX_HEADS), F32),
        jax.ShapeDtypeStruct((BATCH, LP, GLA_K), F32),
        jax.ShapeDtypeStruct((BATCH, LP, GLA_K), F32),
        jax.ShapeDtypeStruct((BATCH, LP, GLA_V), BF16),
        jax.ShapeDtypeStruct((BATCH, LP, GLA_K), F32),
    )
    return pl.pallas_call(
        functools.partial(_inproj_kernel, tm=tm),
        grid=(BATCH, nt),
        in_specs=[row3(D_MODEL), _const_spec(lw["wqkv"].shape), _const_spec(lw["wgla"].shape),
                  _const_spec(lw["ws"].shape), _const_spec(lw["w2"].shape),
                  _const_spec(lw["bs"].shape), _const_spec(lw["ba"].shape), _const_spec(tri.shape)],
        out_specs=(head4, head4, head4, row3(FOX_W), row3(FOX_W), row3(FOX_HEADS),
                   row3(GLA_K), row3(GLA_K), row3(GLA_V), row3(GLA_K)),
        out_shape=out_shape,
        scratch_shapes=[pltpu.VMEM((1, LANES), F32)],
        compiler_params=pltpu.CompilerParams(
            dimension_semantics=("parallel", "arbitrary"), vmem_limit_bytes=VMEM_LIMIT),
        name="prompt_inproj",
    )(hp, lw["wqkv"], lw["wgla"], lw["ws"], lw["w2"], lw["bs"], lw["ba"], tri)


def _fox_kernel(q_ref, k_ref, v_ref, o_ref, vt_sc, acc_sc, m_sc, s0_sc, s1_sc, s2_sc):
    i = pl.program_id(2)
    bq, bk = FOX_BQ, FOX_BLK
    nsub = bq // bk
    half = bk // 2

    @pl.when(i == 0)
    def _():
        for e in range(2):
            def body(c, carry):
                for s in range(2):
                    t = v_ref[0, e, pl.ds(c * bk + s * half, half), :].astype(F32)
                    vt_sc[e, c, :, s * half:(s + 1) * half] = t.T.astype(BF16)
                return carry
            lax.fori_loop(0, LP // bk, body, 0)

    for e in range(2):
        m_sc[e] = jnp.full((1, bq), NEG_INF, F32)
        acc_sc[e] = jnp.zeros((LANES, bq), F32)

    sbuf = (s0_sc, s1_sc, s2_sc)
    causal = (lax.broadcasted_iota(jnp.int32, (bk, 1), 0)
              <= lax.broadcasted_iota(jnp.int32, (1, bq), 1))

    def scores(j, dst, c0):
        start = j * bk if isinstance(j, int) else pl.multiple_of(j * bk, bk)
        for e in range(2):
            dst[e, :, c0:] = _dot_nt(k_ref[0, e, pl.ds(start, bk), :], q_ref[0, e, c0:, :])

    def step(j, cur, c0, masked, next_c0):
        if next_c0 is not None:
            scores(j + 1, sbuf[(cur + 1) % nsub], next_c0)
        for e in range(2):
            st = sbuf[cur][e, :, c0:]
            if masked:
                st = jnp.where(causal[:, :bq - c0], st, NEG_INF)
            m_old = m_sc[e, :, c0:]
            m_new = jnp.maximum(m_old, jnp.max(st, axis=0, keepdims=True))
            alpha = jnp.exp2(m_old - m_new)
            pt = jnp.exp2(st - m_new).astype(BF16)
            acc_sc[e, :, c0:] = alpha * acc_sc[e, :, c0:] + _dot(vt_sc[e, j], pt)
            m_sc[e, :, c0:] = m_new

    scores(0, s0_sc, 0)

    def body(ii, carry):
        for s in range(nsub):
            step(nsub * ii + s, s, 0, False, 0)
        return carry
    lax.fori_loop(0, i, body, 0)

    for s in range(nsub):
        step(nsub * i + s, s, s * bk, True, (s + 1) * bk if s + 1 < nsub else None)

    a0 = acc_sc[0]
    a1 = acc_sc[1]
    row = lax.broadcasted_iota(jnp.int32, (LANES, 1), 0)
    ot = jnp.where(row < FOX_DH, a0 / a0[FOX_DH:FOX_DH + 1, :], a1 / a1[0:1, :])
    o_ref[0] = ot.T.astype(BF16)


def _fox_prompt(q, k, v):
    nq = LP // FOX_BQ
    score_buf = pltpu.VMEM((2, FOX_BLK, FOX_BQ), F32)
    return pl.pallas_call(
        _fox_kernel,
        grid=(BATCH, FOX_HEADS // 2, nq),
        in_specs=[pl.BlockSpec((1, 2, FOX_BQ, LANES), lambda b, p, i: (b, p, i, 0)),
                  pl.BlockSpec((1, 2, LP, LANES), lambda b, p, i: (b, p, 0, 0)),
                  pl.BlockSpec((1, 2, LP, LANES), lambda b, p, i: (b, p, 0, 0))],
        out_specs=pl.BlockSpec((1, FOX_BQ, LANES), lambda b, p, i: (b, i, p)),
        out_shape=jax.ShapeDtypeStruct((BATCH, LP, FOX_W), BF16),
        scratch_shapes=[pltpu.VMEM((2, LP // FOX_BLK, LANES, FOX_BLK), BF16),
                        pltpu.VMEM((2, LANES, FOX_BQ), F32),
                        pltpu.VMEM((2, 1, FOX_BQ), F32),
                        score_buf, score_buf, score_buf],
        compiler_params=pltpu.CompilerParams(
            dimension_semantics=("parallel", "parallel", "arbitrary"), vmem_limit_bytes=VMEM_LIMIT),
        name="fox_prompt",
    )(q, k, v)


def _gla_tables():
    c = GLA_CHUNK
    t = np.arange(c)[:, None]
    j = np.arange(c)[None, :]
    mats = [(j <= t).astype(np.float32), (j > t).astype(np.float32)]
    masks = [np.eye(c, dtype=np.float32)]
    blk = c
    while blk >= 2:
        half = blk // 2
        mid = (t // blk) * blk + half
        mats.append((j <= t).astype(np.float32) - (j <= mid).astype(np.float32))
        s = j
        masks.append((((t // blk) == (s // blk)) & ((t % blk) >= half) & ((s % blk) < half))
                     .astype(np.float32))
        blk = half
    return np.concatenate(mats, axis=0), np.stack([np.concatenate([m, m], axis=0) for m in masks])


_GLA_LEVELS = 7


def _gla_kernel(q_ref, k_ref, lg_ref, v_ref, dall_ref, masks_ref, o_ref, sfin_ref, st_sc):
    i = pl.program_id(2)
    c = GLA_CHUNK

    @pl.when(i == 0)
    def _():
        st_sc[...] = jnp.zeros_like(st_sc)

    lane = lax.broadcasted_iota(jnp.int32, (1, LANES), 1)
    hmask = (lane < GLA_DK, lane >= GLA_DK)

    for ci in range(GLA_TILE // c):
        rows = slice(ci * c, (ci + 1) * c)
        q = q_ref[0, rows, :]
        k = k_ref[0, rows, :]
        lg = lg_ref[0, rows, :]
        hi = lg.astype(BF16)
        lo = (lg - hi.astype(F32)).astype(BF16)
        e2 = _dot(dall_ref[...], jnp.concatenate([hi, lo], axis=1))
        ex = e2[:, :LANES] + e2[:, LANES:]
        bc = ex[0:c]
        q_in = q * jnp.exp(bc)
        k_dec = k * jnp.exp(ex[c:2 * c])
        qs = [q]
        ks = [k.astype(BF16)]
        for lv in range(_GLA_LEVELS):
            f = jnp.exp(-jnp.abs(ex[(2 + lv) * c:(3 + lv) * c]))
            qs.append(q * f)
            ks.append((k * f).astype(BF16))
        decay_all = jnp.exp(bc[c - 1:c])
        a2 = jnp.zeros((2 * c, c), F32)
        for lv in range(_GLA_LEVELS + 1):
            ql = jnp.concatenate([jnp.where(hmask[0], qs[lv], 0.0),
                                  jnp.where(hmask[1], qs[lv], 0.0)], axis=0).astype(BF16)
            a2 = a2 + masks_ref[lv] * _dot_nt(ql, ks[lv])
        for h in range(2):
            a = a2[h * c:(h + 1) * c]
            vh = v_ref[0, rows, h * GLA_DV:(h + 1) * GLA_DV]
            st = st_sc[h]
            o = _dot(a.astype(BF16), vh) + _dot_nt(jnp.where(hmask[h], q_in, 0.0).astype(BF16),
                                                   st.astype(BF16))
            o_ref[0, rows, h * GLA_DV:(h + 1) * GLA_DV] = o
            kd = jnp.where(hmask[h], k_dec, 0.0).astype(BF16)
            st_sc[h] = decay_all * st + _dot_tn(vh, kd)

    @pl.when(i == pl.num_programs(2) - 1)
    def _():
        for h in range(2):
            s = st_sc[h].T
            sfin_ref[0, h] = s[h * GLA_DK:(h + 1) * GLA_DK, :]


def _gla_prompt(gq, gk, lg, gv):
    dall_np, masks_np = _gla_tables()
    dall = jnp.asarray(dall_np, BF16)
    masks = jnp.asarray(masks_np, F32)
    t = GLA_TILE
    pair = pl.BlockSpec((1, t, LANES), lambda b, p, i: (b, i, p))
    wide = pl.BlockSpec((1, t, 2 * GLA_DV), lambda b, p, i: (b, i, p))
    return pl.pallas_call(
        _gla_kernel,
        grid=(BATCH, GLA_HEADS // 2, LP // t),
        in_specs=[pair, pair, pair, wide, _const_spec(dall.shape), _const_spec(masks.shape)],
        out_specs=(wide, pl.BlockSpec((1, 2, GLA_DK, GLA_DV), lambda b, p, i: (b, p, 0, 0))),
        out_shape=(jax.ShapeDtypeStruct((BATCH, LP, GLA_V), F32),
                   jax.ShapeDtypeStruct((BATCH, GLA_HEADS, GLA_DK, GLA_DV), F32)),
        scratch_shapes=[pltpu.VMEM((2, GLA_DV, LANES), F32)],
        compiler_params=pltpu.CompilerParams(
            dimension_semantics=("parallel", "parallel", "arbitrary"), vmem_limit_bytes=VMEM_LIMIT),
        name="gla_prompt",
    )(gq, gk, lg, gv, dall, masks)


def _merge_kernel(h_ref, oa_ref, ob_ref, wg_ref, wpa_ref, wpb_ref, wo_ref, gn_ref, g_ref, b_ref,
                  out_ref):
    x = h_ref[...]
    xb = x.astype(BF16)
    r = _dot(xb, wg_ref[...])
    rb = r[:, :GLA_V]
    ga = r[:, GLA_V:GLA_V + D_MODEL]
    gb = r[:, GLA_V + D_MODEL:]
    ob = ob_ref[...]
    parts = []
    for hd in range(GLA_HEADS):
        o = ob[:, hd * GLA_DV:(hd + 1) * GLA_DV]
        ms = jnp.mean(o * o, axis=-1, keepdims=True)
        parts.append(o * lax.rsqrt(ms + NORM_EPS) * gn_ref[...])
    obn = jnp.concatenate(parts, axis=1) * (rb * _sigmoid(rb))
    y_a = _dot(oa_ref[...], wpa_ref[...])
    y_b = _dot(obn.astype(BF16), wpb_ref[...])
    mixed = _sigmoid(ga) * y_a + _sigmoid(gb) * y_b
    y = ALPHA * x + _dot(mixed.astype(BF16), wo_ref[...])
    out_ref[...] = _layer_norm(y, g_ref[...], b_ref[...])


def _merge(h, oa, ob, lw, tm):
    m = h.shape[0]
    row = lambda w: pl.BlockSpec((tm, w), lambda i: (i, 0))
    ws = (lw["wmg"], lw["wpa"], lw["wpb"], lw["wo"], lw["gn"], lw["ln1g"], lw["ln1b"])
    return pl.pallas_call(
        _merge_kernel,
        grid=(m // tm,),
        in_specs=[row(D_MODEL), row(FOX_W), row(GLA_V)] + [_const_spec(w.shape) for w in ws],
        out_specs=row(D_MODEL),
        out_shape=jax.ShapeDtypeStruct((m, D_MODEL), F32),
        compiler_params=pltpu.CompilerParams(
            dimension_semantics=("parallel",), vmem_limit_bytes=VMEM_LIMIT),
        name="merge",
    )(h, oa, ob, *ws)


def _ffn_kernel(h_ref, wgate_ref, wup_ref, wdown_ref, g_ref, b_ref, out_ref):
    x = h_ref[...]
    xb = x.astype(BF16)
    gt = _dot(xb, wgate_ref[...])
    up = _dot(xb, wup_ref[...])
    hdn = (gt * _sigmoid(gt) * up).astype(BF16)
    y = ALPHA * x + _dot(hdn, wdown_ref[...])
    out_ref[...] = _layer_norm(y, g_ref[...], b_ref[...])


def _ffn(h, lw, tm):
    m = h.shape[0]
    row = pl.BlockSpec((tm, D_MODEL), lambda i: (i, 0))
    ws = (lw["wgate"], lw["wup"], lw["wdown"], lw["ln2g"], lw["ln2b"])
    return pl.pallas_call(
        _ffn_kernel,
        grid=(m // tm,),
        in_specs=[row] + [_const_spec(w.shape) for w in ws],
        out_specs=row,
        out_shape=jax.ShapeDtypeStruct((m, D_MODEL), F32),
        compiler_params=pltpu.CompilerParams(
            dimension_semantics=("parallel",), vmem_limit_bytes=VMEM_LIMIT),
        name="ffn",
    )(h, *ws)


def _sample_inproj_kernel(x_ref, wqkv_ref, wg_ref, ws_ref, w2_ref, bs_ref, ba_ref,
                          r1_ref, r2_ref, lf_ref, lg_ref):
    xb = x_ref[...].astype(BF16)
    r1_ref[...] = _dot(xb, wqkv_ref[...])
    r2_ref[...] = _dot(xb, wg_ref[...])
    rs = _dot(xb, ws_ref[...]) + bs_ref[...]
    lf_ref[...] = _log_sigmoid(rs)
    z = _dot(rs.astype(BF16), w2_ref[...]) + ba_ref[...]
    lg_ref[...] = _log_sigmoid(z) * (1.0 / GLA_TAU)


def _sample_inproj(x, lw):
    ws = (lw["wqkv"], lw["wgla"], lw["ws"], lw["w2"], lw["bs"], lw["ba"])
    full = lambda shape: pl.BlockSpec(shape, lambda i: (0,) * len(shape))
    out_shape = (jax.ShapeDtypeStruct((DEC_BATCH, 3 * FOX_W), F32),
                 jax.ShapeDtypeStruct((DEC_BATCH, 2 * GLA_K + GLA_V), F32),
                 jax.ShapeDtypeStruct((DEC_BATCH, LANES), F32),
                 jax.ShapeDtypeStruct((DEC_BATCH, GLA_K), F32))
    return pl.pallas_call(
        _sample_inproj_kernel,
        grid=(1,),
        in_specs=[full(x.shape)] + [full(w.shape) for w in ws],
        out_specs=tuple(full(s.shape) for s in out_shape),
        out_shape=out_shape,
        compiler_params=pltpu.CompilerParams(vmem_limit_bytes=VMEM_LIMIT),
        name="sample_inproj",
    )(x, *ws)


def _decode_tables():
    j = np.arange(LANES)
    ut = np.concatenate([j[:, None] > j[None, :], np.ones((LANES, LANES), bool)], axis=1)
    nr = PAGES_PER_STEP * FOX_HEADS
    r = np.arange(nr)
    same = (r[:, None] % FOX_HEADS) == (r[None, :] % FOX_HEADS)
    us = np.zeros((nr + 2 * SUBLANES, nr), bool)
    us[:nr] = same & (r[None, :] > r[:, None])
    us[nr:nr + FOX_HEADS] = (r[None, :] % FOX_HEADS) == np.arange(FOX_HEADS)[:, None]
    return ut.astype(np.float32), us.astype(np.float32)


def _fox_decode_kernel(pt_ref, qbd_ref, knb_ref, vrow_ref, lfn_ref, ut_ref, us_ref, *rest):
    npg = PAGES_PER_STEP
    kp = rest[0:npg]
    vp = rest[npg:2 * npg]
    lp = rest[2 * npg:3 * npg]
    o_ref = rest[3 * npg]
    m_sc, l_sc, acc_sc, carry_sc = rest[3 * npg + 1:]
    del pt_ref
    j = pl.program_id(1)
    nh = FOX_HEADS
    nr = npg * nh
    qbd = qbd_ref[0]

    def page2d(ref):
        return ref[...].reshape(FOX_W, PAGE_SIZE).astype(BF16)

    @pl.when(j == 0)
    def _():
        m_sc[...] = _dot(qbd, page2d(knb_ref.at[0]))
        lane = lax.broadcasted_iota(jnp.int32, (nh, LANES), 1)
        l_sc[...] = jnp.where(lane == 0, 1.0, 0.0)
        acc_sc[...] = jnp.broadcast_to(vrow_ref[0].astype(BF16).astype(F32), (nh, FOX_W))
        carry_sc[...] = lfn_ref[0]

    lfc = jnp.concatenate([lp[g][...] for g in range(npg)], axis=0)
    w = _dot(jnp.concatenate(_split3(lfc), axis=0), ut_ref[...])

    s = [_dot(qbd, page2d(kp[g])) for g in range(npg)]

    wsum = w[0:nr] + w[nr:2 * nr] + w[2 * nr:3 * nr]
    x = _dot(us_ref[...], jnp.concatenate(_split3(wsum[:, LANES:]), axis=1))
    xs = x[:, :LANES] + x[:, LANES:2 * LANES] + x[:, 2 * LANES:]
    carry = carry_sc[...]
    sb = [s[g] + (wsum[g * nh:(g + 1) * nh, :LANES] + xs[g * nh:(g + 1) * nh] + carry)
          for g in range(npg)]
    carry_sc[...] = carry + xs[nr:nr + nh]

    mx = sb[0]
    for g in range(1, npg):
        mx = jnp.maximum(mx, sb[g])
    m_old = m_sc[...]
    m_new = jnp.maximum(m_old, jnp.max(mx, axis=1, keepdims=True))
    alpha = jnp.exp(m_old - m_new)
    p = [jnp.exp(sb[g] - m_new) for g in range(npg)]
    psum = p[0]
    for g in range(1, npg):
        psum = psum + p[g]
    l_sc[...] = alpha * l_sc[...] + psum
    m_sc[...] = m_new
    pv = _dot_nt(p[0].astype(BF16), page2d(vp[0]))
    for g in range(1, npg):
        pv = pv + _dot_nt(p[g].astype(BF16), page2d(vp[g]))
    acc_sc[...] = jnp.concatenate([alpha] * (FOX_W // LANES), axis=1) * acc_sc[...] + pv

    @pl.when(j == pl.num_programs(1) - 1)
    def _():
        ltot = jnp.sum(l_sc[...], axis=1, keepdims=True)
        own = (lax.broadcasted_iota(jnp.int32, (nh, FOX_W), 1) // FOX_DH
               == lax.broadcasted_iota(jnp.int32, (nh, FOX_W), 0))
        o_ref[0] = jnp.sum(jnp.where(own, acc_sc[...] / ltot, 0.0), axis=0, keepdims=True)


def _fox_decode(layer, page_table, qbd, knb, vrow, lfn, cache_kt, cache_vt, cache_lft):
    npg = PAGES_PER_STEP
    nsteps = N_PAGES // npg
    ut_np, us_np = _decode_tables()
    ut = jnp.asarray(ut_np, BF16)
    us = jnp.asarray(us_np, BF16)

    def page_map(g, tail):
        def f(b, j, pt):
            return (layer, pt[b, (nsteps - 1 - j) * npg + g]) + tail
        return f

    kv_specs = [pl.BlockSpec((None, None, FOX_HEADS, FOX_DH, PAGE_SIZE), page_map(g, (0, 0, 0)))
                for g in range(npg)]
    lf_specs = [pl.BlockSpec((None, None, FOX_HEADS, PAGE_SIZE), page_map(g, (0, 0)))
                for g in range(npg)]
    per_b = lambda shape: pl.BlockSpec((1,) + shape, lambda b, j, pt: (b,) + (0,) * len(shape))
    const = lambda shape: pl.BlockSpec(shape, lambda b, j, pt: (0,) * len(shape))
    hdl = (FOX_HEADS, FOX_DH, LANES)
    grid_spec = pltpu.PrefetchScalarGridSpec(
        num_scalar_prefetch=1,
        grid=(DEC_BATCH, nsteps),
        in_specs=[per_b((FOX_HEADS, FOX_W)), per_b(hdl), per_b((1, FOX_W)),
                  per_b((FOX_HEADS, LANES)), const(ut.shape), const(us.shape)]
        + kv_specs + kv_specs + lf_specs,
        out_specs=per_b((1, FOX_W)),
        scratch_shapes=[pltpu.VMEM((FOX_HEADS, LANES), F32), pltpu.VMEM((FOX_HEADS, LANES), F32),
                        pltpu.VMEM((FOX_HEADS, FOX_W), F32), pltpu.VMEM((FOX_HEADS, LANES), F32)],
    )
    return pl.pallas_call(
        _fox_decode_kernel,
        grid_spec=grid_spec,
        out_shape=jax.ShapeDtypeStruct((DEC_BATCH, 1, FOX_W), F32),
        compiler_params=pltpu.CompilerParams(
            dimension_semantics=("parallel", "arbitrary"), vmem_limit_bytes=VMEM_LIMIT),
        name="fox_decode",
    )(page_table, qbd, knb, vrow, lfn, ut, us,
      *([cache_kt] * npg), *([cache_vt] * npg), *([cache_lft] * npg))


def _gla_decode_kernel(q_ref, k_ref, g_ref, v_ref, s_ref, o_ref, sn_ref):
    eye = (lax.broadcasted_iota(jnp.int32, (GLA_DK, GLA_DK), 0)
           == lax.broadcasted_iota(jnp.int32, (GLA_DK, GLA_DK), 1))

    def col(r):
        return jnp.sum(jnp.where(eye, jnp.broadcast_to(r, (GLA_DK, GLA_DK)), 0.0),
                       axis=1, keepdims=True)

    for h in range(GLA_HEADS):
        qc = col(q_ref[0, h:h + 1, :])
        kc = col(k_ref[0, h:h + 1, :])
        ac = col(jnp.exp(g_ref[0, h:h + 1, :]))
        sn = ac * s_ref[0, h] + kc * v_ref[0, h:h + 1, :]
        sn_ref[0, h] = sn
        o_ref[0, h:h + 1, :] = jnp.sum(qc * sn, axis=0, keepdims=True)


def _gla_decode(gq, gk, lg, gv, state):
    hk = pl.BlockSpec((1, GLA_HEADS, GLA_DK), lambda b: (b, 0, 0))
    hv = pl.BlockSpec((1, GLA_HEADS, GLA_DV), lambda b: (b, 0, 0))
    st = pl.BlockSpec((1, GLA_HEADS, GLA_DK, GLA_DV), lambda b: (b, 0, 0, 0))
    return pl.pallas_call(
        _gla_decode_kernel,
        grid=(DEC_BATCH,),
        in_specs=[hk, hk, hk, hv, st],
        out_specs=(hv, st),
        out_shape=(jax.ShapeDtypeStruct((DEC_BATCH, GLA_HEADS, GLA_DV), F32),
                   jax.ShapeDtypeStruct((DEC_BATCH, GLA_HEADS, GLA_DK, GLA_DV), F32)),
        compiler_params=pltpu.CompilerParams(
            dimension_semantics=("parallel",), vmem_limit_bytes=VMEM_LIMIT),
        name="gla_decode",
    )(gq, gk, lg, gv, state)


def _layer_weights(l, w_in, b_f, w_a2, b_a, gla_norm_g, w_pa, w_pb, w_o, ln1_g, ln1_b,
                   w_gate, w_up, w_down, ln2_g, ln2_b):
    w = w_in[l]
    o = _OFF
    sc_f = FOX_DH ** -0.5
    sc_g = GLA_DK ** -0.5
    seg = lambda a: w[:, o[a]:o[a + 1]]
    wqkv = jnp.concatenate([seg(0) * sc_f, seg(1), seg(2)], axis=1).astype(BF16)
    wgla = jnp.concatenate([seg(4) * sc_g, seg(5), seg(6)], axis=1).astype(BF16)
    ws = jnp.zeros((D_MODEL, LANES), F32)
    ws = ws.at[:, :FOX_HEADS].set(seg(3)).at[:, FOX_HEADS:FOX_HEADS + GLA_RANK].set(seg(8))
    w2 = jnp.zeros((LANES, GLA_K), F32).at[FOX_HEADS:FOX_HEADS + GLA_RANK].set(w_a2[l])
    bs = jnp.zeros((1, LANES), F32).at[0, :FOX_HEADS].set(b_f[l])
    wmg = jnp.concatenate([seg(7), seg(9), seg(10)], axis=1).astype(BF16)
    return dict(
        wqkv=wqkv, wgla=wgla, ws=ws.astype(BF16), w2=w2.astype(BF16), bs=bs, ba=b_a[l][None],
        wmg=wmg, wpa=w_pa[l].astype(BF16), wpb=w_pb[l].astype(BF16), wo=w_o[l].astype(BF16),
        gn=gla_norm_g[l][None], ln1g=ln1_g[l][None], ln1b=ln1_b[l][None],
        wgate=w_gate[l].astype(BF16), wup=w_up[l].astype(BF16), wdown=w_down[l].astype(BF16),
        ln2g=ln2_g[l][None], ln2b=ln2_b[l][None])


def kernel(x_prompt, x_sample, cache_k, cache_v, cache_lf, state_gla, page_table, meta, w_in, b_f,
           w_a2, b_a, gla_norm_g, w_pa, w_pb, w_o, ln1_g, ln1_b, w_gate, w_up, w_down, ln2_g, ln2_b):
    assert x_prompt.shape == (BATCH, SEQ, D_MODEL) and x_sample.shape == (DEC_BATCH, 1, D_MODEL)
    front = jnp.concatenate([jnp.zeros((PADF, D_MODEL), F32), meta.astype(F32)], axis=0)
    hp = jnp.concatenate([jnp.broadcast_to(front[None], (BATCH, PADF + N_META, D_MODEL)), x_prompt],
                         axis=1)
    hs = x_sample.reshape(DEC_BATCH, D_MODEL)
    cache_kt = jnp.transpose(cache_k, (0, 1, 3, 4, 2))
    cache_vt = jnp.transpose(cache_v, (0, 1, 3, 4, 2))
    cache_lft = jnp.transpose(cache_lf, (0, 1, 3, 2))

    kp, vp, lfp, gp, ksr, vsr, lfs, gs = [], [], [], [], [], [], [], []
    for l in range(DEPTH):
        lw = _layer_weights(l, w_in, b_f, w_a2, b_a, gla_norm_g, w_pa, w_pb, w_o, ln1_g, ln1_b,
                            w_gate, w_up, w_down, ln2_g, ln2_b)
        qa, ka, va, kf, vf, lf, gq, gk, gv, lg = _inproj(hp, lw)
        oa = _fox_prompt(qa, ka, va)
        ob, sfin = _gla_prompt(gq, gk, lg, gv)
        h2 = hp.reshape(BATCH * LP, D_MODEL)
        h2 = _merge(h2, oa.reshape(BATCH * LP, FOX_W), ob.reshape(BATCH * LP, GLA_V), lw, TM_TOK)
        h2 = _ffn(h2, lw, TM_TOK)
        hp = h2.reshape(BATCH, LP, D_MODEL)
        kp.append(kf[:, PADF:].reshape(BATCH, L_REAL, FOX_HEADS, FOX_DH))
        vp.append(vf[:, PADF:].reshape(BATCH, L_REAL, FOX_HEADS, FOX_DH))
        lfp.append(lf[:, PADF:])
        gp.append(sfin)
        r1, r2, lfs_full, lgs = _sample_inproj(hs, lw)
        q_s = r1[:, :FOX_W].reshape(DEC_BATCH, FOX_HEADS, FOX_DH)
        k_s = r1[:, FOX_W:2 * FOX_W].reshape(DEC_BATCH, FOX_HEADS, FOX_DH)
        v_s = r1[:, 2 * FOX_W:].reshape(DEC_BATCH, FOX_HEADS, FOX_DH)
        lf_s = lfs_full[:, :FOX_HEADS]
        lanes = lambda t: jnp.broadcast_to(t[..., None], t.shape + (LANES,))
        eye = jnp.eye(FOX_HEADS, dtype=F32)
        qbd = (q_s[:, :, None, :] * eye[None, :, :, None]).reshape(DEC_BATCH, FOX_HEADS, FOX_W)
        oa_s = _fox_decode(l, page_table, qbd.astype(BF16), lanes(k_s), r1[:, None, 2 * FOX_W:],
                           lanes(lf_s), cache_kt, cache_vt, cache_lft)
        ob_s, s_new = _gla_decode(r2[:, :GLA_K].reshape(DEC_BATCH, GLA_HEADS, GLA_DK),
                                  r2[:, GLA_K:2 * GLA_K].reshape(DEC_BATCH, GLA_HEADS, GLA_DK),
                                  lgs.reshape(DEC_BATCH, GLA_HEADS, GLA_DK),
                                  r2[:, 2 * GLA_K:].reshape(DEC_BATCH, GLA_HEADS, GLA_DV),
                                  state_gla[l])
        hs = _merge(hs, oa_s.reshape(DEC_BATCH, FOX_W).astype(BF16),
                    ob_s.reshape(DEC_BATCH, GLA_V), lw, DEC_BATCH)
        hs = _ffn(hs, lw, DEC_BATCH)
        ksr.append(k_s[:, None])
        vsr.append(v_s[:, None])
        lfs.append(lf_s[:, None])
        gs.append(s_new)

    y_prompt = hp[:, PADF + N_META:]
    y_sample = hs[:, None, :]
    return (y_prompt, y_sample, jnp.stack(kp), jnp.stack(vp), jnp.stack(lfp), jnp.stack(gp),
            jnp.stack(ksr), jnp.stack(vsr), jnp.stack(lfs), jnp.stack(gs))
```

```python
import functools

import numpy as np
import jax
import jax.numpy as jnp
from jax import lax
from jax.experimental import pallas as pl
from jax.experimental.pallas import tpu as pltpu

D_MODEL = 1024
BATCH = 2
SEQ = 8192
DEPTH = 2
DEC_BATCH = 32
PAST_LEN = 8192
PAGE_SIZE = 128
N_META = 16
FOX_HEADS = 8
FOX_DH = 64
FOX_W = FOX_HEADS * FOX_DH
GLA_HEADS = 4
GLA_DK = 64
GLA_DV = 128
GLA_K = GLA_HEADS * GLA_DK
GLA_V = GLA_HEADS * GLA_DV
GLA_RANK = 16
GLA_TAU = 16.0
D_FF = 2816
LN_EPS = 1e-5
NORM_EPS = 1e-6
NEG_INF = -1e30
ALPHA = (2.0 * DEPTH) ** 0.25
_SPLITS = (FOX_W, FOX_W, FOX_W, FOX_HEADS, GLA_K, GLA_K, GLA_V, GLA_V, GLA_RANK, D_MODEL, D_MODEL)
_OFF = np.concatenate([[0], np.cumsum(_SPLITS)]).tolist()

LANES = 128
SUBLANES = 8
VMEM_LIMIT = 56 * 1024 * 1024

L_REAL = SEQ + N_META
FOX_BLK = 256
FOX_BQ = 768
FOX_VROWS = 80
LOG2E = float(np.log2(np.e))
LP = -(-L_REAL // FOX_BLK) * FOX_BLK
PAD_ROWS = LP - L_REAL
GLA_CHUNK = 128
GLA_TILE = 384
TM_IN = 384
CUM_BLK = 128
TM_TOK = 512
N_PAGES = PAST_LEN // PAGE_SIZE
PAGES_PER_STEP = 16

F32 = jnp.float32
BF16 = jnp.bfloat16


def _dot(a, b):
    return jnp.dot(a, b, preferred_element_type=F32)


def _dot_nt(a, b):
    return lax.dot_general(a, b, (((1,), (1,)), ((), ())), preferred_element_type=F32)


def _dot_tn(a, b):
    return lax.dot_general(a, b, (((0,), (0,)), ((), ())), preferred_element_type=F32)


def _log_sigmoid(x):
    return jnp.minimum(x, 0.0) - jnp.log(1.0 + jnp.exp(-jnp.abs(x)))


def _sigmoid(x):
    return 1.0 / (1.0 + jnp.exp(-x))


def _layer_norm(y, g, b):
    mu = jnp.mean(y, axis=-1, keepdims=True)
    d = y - mu
    var = jnp.mean(d * d, axis=-1, keepdims=True)
    return d * lax.rsqrt(var + LN_EPS) * g + b


def _split3(x):
    hi = x.astype(BF16)
    r = x - hi.astype(F32)
    mid = r.astype(BF16)
    lo = (r - mid.astype(F32)).astype(BF16)
    return hi, mid, lo


def _inproj_kernel(x_ref, wqkv_ref, wg_ref, ws_ref, w2_ref, bs_ref, ba_ref, tri_ref,
                   q_ref, k_ref, v_ref, kf_ref, vf_ref, lf_ref, gq_ref, gk_ref, gv_ref, lg_ref,
                   carry_ref, *, tm):
    i = pl.program_id(1)

    @pl.when(i == 0)
    def _():
        carry_ref[...] = jnp.zeros_like(carry_ref)

    row = i * tm + lax.broadcasted_iota(jnp.int32, (tm, 1), 0)
    real = row < L_REAL
    xb = jnp.where(real, x_ref[0], 0.0).astype(BF16)

    rs = _dot(xb, ws_ref[...]) + bs_ref[...]
    lf_full = jnp.where(real, _log_sigmoid(rs), 0.0)
    lf_ref[0] = lf_full[:, :FOX_HEADS]
    z = _dot(rs.astype(BF16), w2_ref[...]) + ba_ref[...]
    lg_ref[0] = jnp.where(real, _log_sigmoid(z) * (1.0 / GLA_TAU), 0.0)

    carry = carry_ref[...]
    tri = tri_ref[...]
    cs = []
    for sb in range(tm // CUM_BLK):
        hi, mid, lo = _split3(lf_full[sb * CUM_BLK:(sb + 1) * CUM_BLK])
        c = _dot(tri, hi) + _dot(tri, mid) + _dot(tri, lo) + carry
        carry = c[CUM_BLK - 1:CUM_BLK]
        cs.append(c)
    carry_ref[...] = carry
    c = jnp.concatenate(cs, axis=0) * LOG2E

    r = _dot(xb, wqkv_ref[...])
    kf_ref[0] = r[:, FOX_W:2 * FOX_W].T
    vf_ref[0] = r[:, 2 * FOX_W:3 * FOX_W].T

    lane = lax.broadcasted_iota(jnp.int32, (1, LANES), 1)
    for h in range(FOX_HEADS):
        p, e = divmod(h, 2)
        dmask = (lane < FOX_DH) if e == 0 else (lane >= FOX_DH)
        xo = FOX_DH if e == 0 else 0
        ch = jnp.broadcast_to(c[:, h:h + 1], (tm, LANES))
        hi = ch.astype(BF16).astype(F32)
        r1 = ch - hi
        mid = r1.astype(BF16).astype(F32)
        lo = r1 - mid
        one3 = (lane >= xo + 3) & (lane < xo + 6)
        eq = jnp.where(lane == xo, hi, jnp.where(lane == xo + 1, mid, jnp.where(
            lane == xo + 2, lo, jnp.where(one3, 1.0, 0.0))))
        first3 = (lane >= xo) & (lane < xo + 3)
        ek = jnp.where(first3, 1.0, jnp.where(lane == xo + 3, -hi, jnp.where(
            lane == xo + 4, -mid, jnp.where(lane == xo + 5, -lo, 0.0))))
        ev = jnp.where(lane == xo, 1.0, 0.0)
        rq = r[:, p * LANES:(p + 1) * LANES]
        rk = r[:, FOX_W + p * LANES:FOX_W + (p + 1) * LANES]
        rv = r[:, 2 * FOX_W + p * LANES:2 * FOX_W + (p + 1) * LANES]
        q_ref[0, h] = jnp.where(dmask, rq * LOG2E, eq).astype(BF16)
        k_ref[0, h] = jnp.where(dmask, rk, ek).astype(BF16)
        v_ref[0, h] = jnp.where(dmask, rv, ev).astype(BF16)

    rg = _dot(xb, wg_ref[...])
    gq_ref[0] = rg[:, :GLA_K]
    gk_ref[0] = rg[:, GLA_K:2 * GLA_K]
    gv_ref[0] = rg[:, 2 * GLA_K:].astype(BF16)


def _const_spec(shape):
    nd = len(shape)
    return pl.BlockSpec(shape, lambda *_: (0,) * nd, pipeline_mode=pl.Buffered(1))


def _inproj(hp, lw):
    tm = TM_IN
    nt = LP // tm
    tri = jnp.asarray(np.tril(np.ones((CUM_BLK, CUM_BLK), np.float32)), BF16)
    row3 = lambda w: pl.BlockSpec((1, tm, w), lambda b, i: (b, i, 0))
    head4 = pl.BlockSpec((1, FOX_HEADS, tm, LANES), lambda b, i: (b, 0, i, 0))
    col3 = pl.BlockSpec((1, FOX_W, tm), lambda b, i: (b, 0, i))
    out_shape = (
        jax.ShapeDtypeStruct((BATCH, FOX_HEADS, LP, LANES), BF16),
        jax.ShapeDtypeStruct((BATCH, FOX_HEADS, LP, LANES), BF16),
        jax.ShapeDtypeStruct((BATCH, FOX_HEADS, LP, LANES), BF16),
        jax.ShapeDtypeStruct((BATCH, FOX_W, L_REAL), F32),
        jax.ShapeDtypeStruct((BATCH, FOX_W, L_REAL), F32),
        jax.ShapeDtypeStruct((BATCH, LP, FOX_HEADS), F32),
        jax.ShapeDtypeStruct((BATCH, LP, GLA_K), F32),
        jax.ShapeDtypeStruct((BATCH, LP, GLA_K), F32),
        jax.ShapeDtypeStruct((BATCH, LP, GLA_V), BF16),
        jax.ShapeDtypeStruct((BATCH, LP, GLA_K), F32),
    )
    return pl.pallas_call(
        functools.partial(_inproj_kernel, tm=tm),
        grid=(BATCH, nt),
        in_specs=[row3(D_MODEL), _const_spec(lw["wqkv"].shape), _const_spec(lw["wgla"].shape),
                  _const_spec(lw["ws"].shape), _const_spec(lw["w2"].shape),
                  _const_spec(lw["bs"].shape), _const_spec(lw["ba"].shape), _const_spec(tri.shape)],
        out_specs=(head4, head4, head4, col3, col3, row3(FOX_HEADS),
                   row3(GLA_K), row3(GLA_K), row3(GLA_V), row3(GLA_K)),
        out_shape=out_shape,
        scratch_shapes=[pltpu.VMEM((1, LANES), F32)],
        compiler_params=pltpu.CompilerParams(
            dimension_semantics=("parallel", "arbitrary"), vmem_limit_bytes=VMEM_LIMIT),
        name="prompt_inproj",
    )(hp, lw["wqkv"], lw["wgla"], lw["ws"], lw["w2"], lw["bs"], lw["ba"], tri)


def _fox_kernel(q_ref, k_ref, v_ref, o_ref, vt_sc, acc_sc, m_sc, s0_sc, s1_sc, s2_sc):
    i = pl.program_id(2)
    bq, bk = FOX_BQ, FOX_BLK
    nsub = bq // bk
    half = bk // 2

    @pl.when(i == 0)
    def _():
        for e in range(2):
            def body(c, carry):
                for s in range(2):
                    t = v_ref[0, e, pl.ds(c * bk + s * half, half), :].astype(F32)
                    tt = t.T.astype(BF16)
                    cols = slice(s * half, (s + 1) * half)
                    if e == 0:
                        vt_sc[e, c, :, cols] = tt[:FOX_VROWS]
                    else:
                        vt_sc[e, c, :FOX_DH, cols] = tt[FOX_DH:]
                        vt_sc[e, c, FOX_DH:, cols] = tt[:FOX_VROWS - FOX_DH]
                return carry
            lax.fori_loop(0, LP // bk, body, 0)

    for e in range(2):
        m_sc[e] = jnp.full((1, bq), NEG_INF, F32)
        acc_sc[e] = jnp.zeros((FOX_VROWS, bq), F32)

    sbuf = (s0_sc, s1_sc, s2_sc)
    causal = (lax.broadcasted_iota(jnp.int32, (bk, 1), 0)
              <= lax.broadcasted_iota(jnp.int32, (1, bq), 1))

    def scores(j, dst, c0):
        start = j * bk if isinstance(j, int) else pl.multiple_of(j * bk, bk)
        for e in range(2):
            dst[e, :, c0:] = _dot_nt(k_ref[0, e, pl.ds(start, bk), :], q_ref[0, e, c0:, :])

    def step(j, cur, c0, masked, next_c0):
        if next_c0 is not None:
            scores(j + 1, sbuf[(cur + 1) % nsub], next_c0)
        for e in range(2):
            st = sbuf[cur][e, :, c0:]
            if masked:
                st = jnp.where(causal[:, :bq - c0], st, NEG_INF)
            m_old = m_sc[e, :, c0:]
            m_new = jnp.maximum(m_old, jnp.max(st, axis=0, keepdims=True))
            alpha = jnp.exp2(m_old - m_new)
            pt = jnp.exp2(st - m_new).astype(BF16)
            acc_sc[e, :, c0:] = alpha * acc_sc[e, :, c0:] + _dot(vt_sc[e, j], pt)
            m_sc[e, :, c0:] = m_new

    scores(0, s0_sc, 0)

    def body(ii, carry):
        for s in range(nsub):
            step(nsub * ii + s, s, 0, False, 0)
        return carry
    lax.fori_loop(0, i, body, 0)

    for s in range(nsub):
        step(nsub * i + s, s, s * bk, True, (s + 1) * bk if s + 1 < nsub else None)

    ot = jnp.concatenate([acc_sc[e, :FOX_DH, :] / acc_sc[e, FOX_DH:FOX_DH + 1, :] for e in range(2)],
                         axis=0)
    o_ref[0] = ot.T.astype(BF16)


def _fox_prompt(q, k, v):
    nq = LP // FOX_BQ
    score_buf = pltpu.VMEM((2, FOX_BLK, FOX_BQ), F32)
    return pl.pallas_call(
        _fox_kernel,
        grid=(BATCH, FOX_HEADS // 2, nq),
        in_specs=[pl.BlockSpec((1, 2, FOX_BQ, LANES), lambda b, p, i: (b, p, i, 0)),
                  pl.BlockSpec((1, 2, LP, LANES), lambda b, p, i: (b, p, 0, 0)),
                  pl.BlockSpec((1, 2, LP, LANES), lambda b, p, i: (b, p, 0, 0))],
        out_specs=pl.BlockSpec((1, FOX_BQ, LANES), lambda b, p, i: (b, i, p)),
        out_shape=jax.ShapeDtypeStruct((BATCH, LP, FOX_W), BF16),
        scratch_shapes=[pltpu.VMEM((2, LP // FOX_BLK, FOX_VROWS, FOX_BLK), BF16),
                        pltpu.VMEM((2, FOX_VROWS, FOX_BQ), F32),
                        pltpu.VMEM((2, 1, FOX_BQ), F32),
                        score_buf, score_buf, score_buf],
        compiler_params=pltpu.CompilerParams(
            dimension_semantics=("parallel", "parallel", "arbitrary"), vmem_limit_bytes=VMEM_LIMIT),
        name="fox_prompt",
    )(q, k, v)


def _gla_tables():
    c = GLA_CHUNK
    t = np.arange(c)[:, None]
    j = np.arange(c)[None, :]
    mats = [(j <= t).astype(np.float32), (j > t).astype(np.float32)]
    masks = [np.eye(c, dtype=np.float32)]
    blk = c
    while blk >= 2:
        half = blk // 2
        mid = (t // blk) * blk + half
        mats.append((j <= t).astype(np.float32) - (j <= mid).astype(np.float32))
        s = j
        masks.append((((t // blk) == (s // blk)) & ((t % blk) >= half) & ((s % blk) < half))
                     .astype(np.float32))
        blk = half
    return np.concatenate(mats, axis=0), np.stack([np.concatenate([m, m], axis=0) for m in masks])


_GLA_LEVELS = 7


def _gla_kernel(q_ref, k_ref, lg_ref, v_ref, dall_ref, masks_ref, o_ref, sfin_ref, st_sc):
    i = pl.program_id(2)
    c = GLA_CHUNK

    @pl.when(i == 0)
    def _():
        st_sc[...] = jnp.zeros_like(st_sc)

    lane = lax.broadcasted_iota(jnp.int32, (1, LANES), 1)
    hmask = (lane < GLA_DK, lane >= GLA_DK)

    for ci in range(GLA_TILE // c):
        rows = slice(ci * c, (ci + 1) * c)
        q = q_ref[0, rows, :]
        k = k_ref[0, rows, :]
        lg = lg_ref[0, rows, :]
        hi = lg.astype(BF16)
        lo = (lg - hi.astype(F32)).astype(BF16)
        e2 = _dot(dall_ref[...], jnp.concatenate([hi, lo], axis=1))
        ex = e2[:, :LANES] + e2[:, LANES:]
        bc = ex[0:c]
        q_in = q * jnp.exp(bc)
        k_dec = k * jnp.exp(ex[c:2 * c])
        qs = [q]
        ks = [k.astype(BF16)]
        for lv in range(_GLA_LEVELS):
            f = jnp.exp(-jnp.abs(ex[(2 + lv) * c:(3 + lv) * c]))
            qs.append(q * f)
            ks.append((k * f).astype(BF16))
        decay_all = jnp.exp(bc[c - 1:c])
        a2 = jnp.zeros((2 * c, c), F32)
        for lv in range(_GLA_LEVELS + 1):
            ql = jnp.concatenate([jnp.where(hmask[0], qs[lv], 0.0),
                                  jnp.where(hmask[1], qs[lv], 0.0)], axis=0).astype(BF16)
            a2 = a2 + masks_ref[lv] * _dot_nt(ql, ks[lv])
        for h in range(2):
            a = a2[h * c:(h + 1) * c]
            vh = v_ref[0, rows, h * GLA_DV:(h + 1) * GLA_DV]
            st = st_sc[h]
            o = _dot(a.astype(BF16), vh) + _dot_nt(jnp.where(hmask[h], q_in, 0.0).astype(BF16),
                                                   st.astype(BF16))
            o_ref[0, rows, h * GLA_DV:(h + 1) * GLA_DV] = o
            kd = jnp.where(hmask[h], k_dec, 0.0).astype(BF16)
            st_sc[h] = decay_all * st + _dot_tn(vh, kd)

    @pl.when(i == pl.num_programs(2) - 1)
    def _():
        for h in range(2):
            s = st_sc[h].T
            sfin_ref[0, h] = s[h * GLA_DK:(h + 1) * GLA_DK, :]


def _gla_prompt(gq, gk, lg, gv):
    dall_np, masks_np = _gla_tables()
    dall = jnp.asarray(dall_np, BF16)
    masks = jnp.asarray(masks_np, F32)
    t = GLA_TILE
    pair = pl.BlockSpec((1, t, LANES), lambda b, p, i: (b, i, p))
    wide = pl.BlockSpec((1, t, 2 * GLA_DV), lambda b, p, i: (b, i, p))
    return pl.pallas_call(
        _gla_kernel,
        grid=(BATCH, GLA_HEADS // 2, LP // t),
        in_specs=[pair, pair, pair, wide, _const_spec(dall.shape), _const_spec(masks.shape)],
        out_specs=(wide, pl.BlockSpec((1, 2, GLA_DK, GLA_DV), lambda b, p, i: (b, p, 0, 0))),
        out_shape=(jax.ShapeDtypeStruct((BATCH, LP, GLA_V), F32),
                   jax.ShapeDtypeStruct((BATCH, GLA_HEADS, GLA_DK, GLA_DV), F32)),
        scratch_shapes=[pltpu.VMEM((2, GLA_DV, LANES), F32)],
        compiler_params=pltpu.CompilerParams(
            dimension_semantics=("parallel", "parallel", "arbitrary"), vmem_limit_bytes=VMEM_LIMIT),
        name="gla_prompt",
    )(gq, gk, lg, gv, dall, masks)


def _merge_kernel(h_ref, oa_ref, ob_ref, wg_ref, wpa_ref, wpb_ref, wo_ref, gn_ref, g_ref, b_ref,
                  out_ref):
    x = h_ref[...]
    xb = x.astype(BF16)
    r = _dot(xb, wg_ref[...])
    rb = r[:, :GLA_V]
    ga = r[:, GLA_V:GLA_V + D_MODEL]
    gb = r[:, GLA_V + D_MODEL:]
    ob = ob_ref[...]
    parts = []
    for hd in range(GLA_HEADS):
        o = ob[:, hd * GLA_DV:(hd + 1) * GLA_DV]
        ms = jnp.mean(o * o, axis=-1, keepdims=True)
        parts.append(o * lax.rsqrt(ms + NORM_EPS) * gn_ref[...])
    obn = jnp.concatenate(parts, axis=1) * (rb * _sigmoid(rb))
    y_a = _dot(oa_ref[...], wpa_ref[...])
    y_b = _dot(obn.astype(BF16), wpb_ref[...])
    mixed = _sigmoid(ga) * y_a + _sigmoid(gb) * y_b
    y = ALPHA * x + _dot(mixed.astype(BF16), wo_ref[...])
    out_ref[...] = _layer_norm(y, g_ref[...], b_ref[...])


def _merge(h, oa, ob, lw, tm):
    m = h.shape[0]
    row = lambda w: pl.BlockSpec((tm, w), lambda i: (i, 0))
    ws = (lw["wmg"], lw["wpa"], lw["wpb"], lw["wo"], lw["gn"], lw["ln1g"], lw["ln1b"])
    return pl.pallas_call(
        _merge_kernel,
        grid=(m // tm,),
        in_specs=[row(D_MODEL), row(FOX_W), row(GLA_V)] + [_const_spec(w.shape) for w in ws],
        out_specs=row(D_MODEL),
        out_shape=jax.ShapeDtypeStruct((m, D_MODEL), F32),
        compiler_params=pltpu.CompilerParams(
            dimension_semantics=("parallel",), vmem_limit_bytes=VMEM_LIMIT),
        name="merge",
    )(h, oa, ob, *ws)


def _ffn_kernel(h_ref, wgate_ref, wup_ref, wdown_ref, g_ref, b_ref, out_ref):
    x = h_ref[...]
    xb = x.astype(BF16)
    gt = _dot(xb, wgate_ref[...])
    up = _dot(xb, wup_ref[...])
    hdn = (gt * _sigmoid(gt) * up).astype(BF16)
    y = ALPHA * x + _dot(hdn, wdown_ref[...])
    out_ref[...] = _layer_norm(y, g_ref[...], b_ref[...])


def _ffn(h, lw, tm):
    m = h.shape[0]
    row = pl.BlockSpec((tm, D_MODEL), lambda i: (i, 0))
    ws = (lw["wgate"], lw["wup"], lw["wdown"], lw["ln2g"], lw["ln2b"])
    return pl.pallas_call(
        _ffn_kernel,
        grid=(m // tm,),
        in_specs=[row] + [_const_spec(w.shape) for w in ws],
        out_specs=row,
        out_shape=jax.ShapeDtypeStruct((m, D_MODEL), F32),
        compiler_params=pltpu.CompilerParams(
            dimension_semantics=("parallel",), vmem_limit_bytes=VMEM_LIMIT),
        name="ffn",
    )(h, *ws)


def _sample_inproj_kernel(x_ref, wqkv_ref, wg_ref, ws_ref, w2_ref, bs_ref, ba_ref,
                          r1_ref, r2_ref, lf_ref, lg_ref):
    xb = x_ref[...].astype(BF16)
    r1_ref[...] = _dot(xb, wqkv_ref[...])
    r2_ref[...] = _dot(xb, wg_ref[...])
    rs = _dot(xb, ws_ref[...]) + bs_ref[...]
    lf_ref[...] = _log_sigmoid(rs)
    z = _dot(rs.astype(BF16), w2_ref[...]) + ba_ref[...]
    lg_ref[...] = _log_sigmoid(z) * (1.0 / GLA_TAU)


def _sample_inproj(x, lw):
    ws = (lw["wqkv"], lw["wgla"], lw["ws"], lw["w2"], lw["bs"], lw["ba"])
    full = lambda shape: pl.BlockSpec(shape, lambda i: (0,) * len(shape))
    out_shape = (jax.ShapeDtypeStruct((DEC_BATCH, 3 * FOX_W), F32),
                 jax.ShapeDtypeStruct((DEC_BATCH, 2 * GLA_K + GLA_V), F32),
                 jax.ShapeDtypeStruct((DEC_BATCH, LANES), F32),
                 jax.ShapeDtypeStruct((DEC_BATCH, GLA_K), F32))
    return pl.pallas_call(
        _sample_inproj_kernel,
        grid=(1,),
        in_specs=[full(x.shape)] + [full(w.shape) for w in ws],
        out_specs=tuple(full(s.shape) for s in out_shape),
        out_shape=out_shape,
        compiler_params=pltpu.CompilerParams(vmem_limit_bytes=VMEM_LIMIT),
        name="sample_inproj",
    )(x, *ws)


def _decode_tables():
    j = np.arange(LANES)
    ut = np.concatenate([j[:, None] > j[None, :], np.ones((LANES, LANES), bool)], axis=1)
    nr = PAGES_PER_STEP * FOX_HEADS
    r = np.arange(nr)
    same = (r[:, None] % FOX_HEADS) == (r[None, :] % FOX_HEADS)
    us = np.zeros((nr + 2 * SUBLANES, nr), bool)
    us[:nr] = same & (r[None, :] > r[:, None])
    us[nr:nr + FOX_HEADS] = (r[None, :] % FOX_HEADS) == np.arange(FOX_HEADS)[:, None]
    return ut.astype(np.float32), us.astype(np.float32)


def _fox_decode_kernel(pt_ref, qbd_ref, knb_ref, vrow_ref, lfn_ref, ut_ref, us_ref, *rest):
    npg = PAGES_PER_STEP
    kp = rest[0:npg]
    vp = rest[npg:2 * npg]
    lp = rest[2 * npg:3 * npg]
    o_ref = rest[3 * npg]
    m_sc, l_sc, acc_sc, carry_sc = rest[3 * npg + 1:]
    del pt_ref
    j = pl.program_id(1)
    nh = FOX_HEADS
    nr = npg * nh
    qbd = qbd_ref[0]

    def page2d(ref):
        return ref[...].reshape(FOX_W, PAGE_SIZE).astype(BF16)

    @pl.when(j == 0)
    def _():
        m_sc[...] = _dot(qbd, page2d(knb_ref.at[0]))
        lane = lax.broadcasted_iota(jnp.int32, (nh, LANES), 1)
        l_sc[...] = jnp.where(lane == 0, 1.0, 0.0)
        acc_sc[...] = jnp.broadcast_to(vrow_ref[0].astype(BF16).astype(F32), (nh, FOX_W))
        carry_sc[...] = lfn_ref[0]

    lfc = jnp.concatenate([lp[g][...] for g in range(npg)], axis=0)
    w = _dot(jnp.concatenate(_split3(lfc), axis=0), ut_ref[...])

    s = [_dot(qbd, page2d(kp[g])) for g in range(npg)]

    wsum = w[0:nr] + w[nr:2 * nr] + w[2 * nr:3 * nr]
    x = _dot(us_ref[...], jnp.concatenate(_split3(wsum[:, LANES:]), axis=1))
    xs = x[:, :LANES] + x[:, LANES:2 * LANES] + x[:, 2 * LANES:]
    carry = carry_sc[...]
    sb = [s[g] + (wsum[g * nh:(g + 1) * nh, :LANES] + xs[g * nh:(g + 1) * nh] + carry)
          for g in range(npg)]
    carry_sc[...] = carry + xs[nr:nr + nh]

    mx = sb[0]
    for g in range(1, npg):
        mx = jnp.maximum(mx, sb[g])
    m_old = m_sc[...]
    m_new = jnp.maximum(m_old, jnp.max(mx, axis=1, keepdims=True))
    alpha = jnp.exp(m_old - m_new)
    p = [jnp.exp(sb[g] - m_new) for g in range(npg)]
    psum = p[0]
    for g in range(1, npg):
        psum = psum + p[g]
    l_sc[...] = alpha * l_sc[...] + psum
    m_sc[...] = m_new
    pv = _dot_nt(p[0].astype(BF16), page2d(vp[0]))
    for g in range(1, npg):
        pv = pv + _dot_nt(p[g].astype(BF16), page2d(vp[g]))
    acc_sc[...] = jnp.concatenate([alpha] * (FOX_W // LANES), axis=1) * acc_sc[...] + pv

    @pl.when(j == pl.num_programs(1) - 1)
    def _():
        ltot = jnp.sum(l_sc[...], axis=1, keepdims=True)
        own = (lax.broadcasted_iota(jnp.int32, (nh, FOX_W), 1) // FOX_DH
               == lax.broadcasted_iota(jnp.int32, (nh, FOX_W), 0))
        o_ref[0] = jnp.sum(jnp.where(own, acc_sc[...] / ltot, 0.0), axis=0, keepdims=True)


def _fox_decode(layer, page_table, qbd, knb, vrow, lfn, cache_kt, cache_vt, cache_lft):
    npg = PAGES_PER_STEP
    nsteps = N_PAGES // npg
    ut_np, us_np = _decode_tables()
    ut = jnp.asarray(ut_np, BF16)
    us = jnp.asarray(us_np, BF16)

    def page_map(g, tail):
        def f(b, j, pt):
            return (layer, pt[b, (nsteps - 1 - j) * npg + g]) + tail
        return f

    kv_specs = [pl.BlockSpec((None, None, FOX_HEADS, FOX_DH, PAGE_SIZE), page_map(g, (0, 0, 0)))
                for g in range(npg)]
    lf_specs = [pl.BlockSpec((None, None, FOX_HEADS, PAGE_SIZE), page_map(g, (0, 0)))
                for g in range(npg)]
    per_b = lambda shape: pl.BlockSpec((1,) + shape, lambda b, j, pt: (b,) + (0,) * len(shape))
    const = lambda shape: pl.BlockSpec(shape, lambda b, j, pt: (0,) * len(shape))
    hdl = (FOX_HEADS, FOX_DH, LANES)
    grid_spec = pltpu.PrefetchScalarGridSpec(
        num_scalar_prefetch=1,
        grid=(DEC_BATCH, nsteps),
        in_specs=[per_b((FOX_HEADS, FOX_W)), per_b(hdl), per_b((1, FOX_W)),
                  per_b((FOX_HEADS, LANES)), const(ut.shape), const(us.shape)]
        + kv_specs + kv_specs + lf_specs,
        out_specs=per_b((1, FOX_W)),
        scratch_shapes=[pltpu.VMEM((FOX_HEADS, LANES), F32), pltpu.VMEM((FOX_HEADS, LANES), F32),
                        pltpu.VMEM((FOX_HEADS, FOX_W), F32), pltpu.VMEM((FOX_HEADS, LANES), F32)],
    )
    return pl.pallas_call(
        _fox_decode_kernel,
        grid_spec=grid_spec,
        out_shape=jax.ShapeDtypeStruct((DEC_BATCH, 1, FOX_W), F32),
        compiler_params=pltpu.CompilerParams(
            dimension_semantics=("parallel", "arbitrary"), vmem_limit_bytes=VMEM_LIMIT),
        name="fox_decode",
    )(page_table, qbd, knb, vrow, lfn, ut, us,
      *([cache_kt] * npg), *([cache_vt] * npg), *([cache_lft] * npg))


def _gla_decode_kernel(q_ref, k_ref, g_ref, v_ref, s_ref, o_ref, sn_ref):
    eye = (lax.broadcasted_iota(jnp.int32, (GLA_DK, GLA_DK), 0)
           == lax.broadcasted_iota(jnp.int32, (GLA_DK, GLA_DK), 1))

    def col(r):
        return jnp.sum(jnp.where(eye, jnp.broadcast_to(r, (GLA_DK, GLA_DK)), 0.0),
                       axis=1, keepdims=True)

    for h in range(GLA_HEADS):
        qc = col(q_ref[0, h:h + 1, :])
        kc = col(k_ref[0, h:h + 1, :])
        ac = col(jnp.exp(g_ref[0, h:h + 1, :]))
        sn = ac * s_ref[0, h] + kc * v_ref[0, h:h + 1, :]
        sn_ref[0, h] = sn
        o_ref[0, h:h + 1, :] = jnp.sum(qc * sn, axis=0, keepdims=True)


def _gla_decode(gq, gk, lg, gv, state):
    hk = pl.BlockSpec((1, GLA_HEADS, GLA_DK), lambda b: (b, 0, 0))
    hv = pl.BlockSpec((1, GLA_HEADS, GLA_DV), lambda b: (b, 0, 0))
    st = pl.BlockSpec((1, GLA_HEADS, GLA_DK, GLA_DV), lambda b: (b, 0, 0, 0))
    return pl.pallas_call(
        _gla_decode_kernel,
        grid=(DEC_BATCH,),
        in_specs=[hk, hk, hk, hv, st],
        out_specs=(hv, st),
        out_shape=(jax.ShapeDtypeStruct((DEC_BATCH, GLA_HEADS, GLA_DV), F32),
                   jax.ShapeDtypeStruct((DEC_BATCH, GLA_HEADS, GLA_DK, GLA_DV), F32)),
        compiler_params=pltpu.CompilerParams(
            dimension_semantics=("parallel",), vmem_limit_bytes=VMEM_LIMIT),
        name="gla_decode",
    )(gq, gk, lg, gv, state)


def _layer_weights(l, w_in, b_f, w_a2, b_a, gla_norm_g, w_pa, w_pb, w_o, ln1_g, ln1_b,
                   w_gate, w_up, w_down, ln2_g, ln2_b):
    w = w_in[l]
    o = _OFF
    sc_f = FOX_DH ** -0.5
    sc_g = GLA_DK ** -0.5
    seg = lambda a: w[:, o[a]:o[a + 1]]
    wqkv = jnp.concatenate([seg(0) * sc_f, seg(1), seg(2)], axis=1).astype(BF16)
    wgla = jnp.concatenate([seg(4) * sc_g, seg(5), seg(6)], axis=1).astype(BF16)
    ws = jnp.zeros((D_MODEL, LANES), F32)
    ws = ws.at[:, :FOX_HEADS].set(seg(3)).at[:, FOX_HEADS:FOX_HEADS + GLA_RANK].set(seg(8))
    w2 = jnp.zeros((LANES, GLA_K), F32).at[FOX_HEADS:FOX_HEADS + GLA_RANK].set(w_a2[l])
    bs = jnp.zeros((1, LANES), F32).at[0, :FOX_HEADS].set(b_f[l])
    wmg = jnp.concatenate([seg(7), seg(9), seg(10)], axis=1).astype(BF16)
    return dict(
        wqkv=wqkv, wgla=wgla, ws=ws.astype(BF16), w2=w2.astype(BF16), bs=bs, ba=b_a[l][None],
        wmg=wmg, wpa=w_pa[l].astype(BF16), wpb=w_pb[l].astype(BF16), wo=w_o[l].astype(BF16),
        gn=gla_norm_g[l][None], ln1g=ln1_g[l][None], ln1b=ln1_b[l][None],
        wgate=w_gate[l].astype(BF16), wup=w_up[l].astype(BF16), wdown=w_down[l].astype(BF16),
        ln2g=ln2_g[l][None], ln2b=ln2_b[l][None])


def kernel(x_prompt, x_sample, cache_k, cache_v, cache_lf, state_gla, page_table, meta, w_in, b_f,
           w_a2, b_a, gla_norm_g, w_pa, w_pb, w_o, ln1_g, ln1_b, w_gate, w_up, w_down, ln2_g, ln2_b):
    assert x_prompt.shape == (BATCH, SEQ, D_MODEL) and x_sample.shape == (DEC_BATCH, 1, D_MODEL)
    hp = jnp.concatenate([jnp.broadcast_to(meta.astype(F32)[None], (BATCH, N_META, D_MODEL)), x_prompt,
                          jnp.zeros((BATCH, PAD_ROWS, D_MODEL), F32)], axis=1)
    hs = x_sample.reshape(DEC_BATCH, D_MODEL)
    cache_kt = jnp.transpose(cache_k, (0, 1, 3, 4, 2))
    cache_vt = jnp.transpose(cache_v, (0, 1, 3, 4, 2))
    cache_lft = jnp.transpose(cache_lf, (0, 1, 3, 2))

    kp, vp, lfp, gp, ksr, vsr, lfs, gs = [], [], [], [], [], [], [], []
    for l in range(DEPTH):
        lw = _layer_weights(l, w_in, b_f, w_a2, b_a, gla_norm_g, w_pa, w_pb, w_o, ln1_g, ln1_b,
                            w_gate, w_up, w_down, ln2_g, ln2_b)
        qa, ka, va, kf, vf, lf, gq, gk, gv, lg = _inproj(hp, lw)
        oa = _fox_prompt(qa, ka, va)
        ob, sfin = _gla_prompt(gq, gk, lg, gv)
        h2 = hp.reshape(BATCH * LP, D_MODEL)
        h2 = _merge(h2, oa.reshape(BATCH * LP, FOX_W), ob.reshape(BATCH * LP, GLA_V), lw, TM_TOK)
        h2 = _ffn(h2, lw, TM_TOK)
        hp = h2.reshape(BATCH, LP, D_MODEL)
        kp.append(kf)
        vp.append(vf)
        lfp.append(lf[:, :L_REAL])
        gp.append(sfin)
        r1, r2, lfs_full, lgs = _sample_inproj(hs, lw)
        q_s = r1[:, :FOX_W].reshape(DEC_BATCH, FOX_HEADS, FOX_DH)
        k_s = r1[:, FOX_W:2 * FOX_W].reshape(DEC_BATCH, FOX_HEADS, FOX_DH)
        v_s = r1[:, 2 * FOX_W:].reshape(DEC_BATCH, FOX_HEADS, FOX_DH)
        lf_s = lfs_full[:, :FOX_HEADS]
        lanes = lambda t: jnp.broadcast_to(t[..., None], t.shape + (LANES,))
        eye = jnp.eye(FOX_HEADS, dtype=F32)
        qbd = (q_s[:, :, None, :] * eye[None, :, :, None]).reshape(DEC_BATCH, FOX_HEADS, FOX_W)
        oa_s = _fox_decode(l, page_table, qbd.astype(BF16), lanes(k_s), r1[:, None, 2 * FOX_W:],
                           lanes(lf_s), cache_kt, cache_vt, cache_lft)
        ob_s, s_new = _gla_decode(r2[:, :GLA_K].reshape(DEC_BATCH, GLA_HEADS, GLA_DK),
                                  r2[:, GLA_K:2 * GLA_K].reshape(DEC_BATCH, GLA_HEADS, GLA_DK),
                                  lgs.reshape(DEC_BATCH, GLA_HEADS, GLA_DK),
                                  r2[:, 2 * GLA_K:].reshape(DEC_BATCH, GLA_HEADS, GLA_DV),
                                  state_gla[l])
        hs = _merge(hs, oa_s.reshape(DEC_BATCH, FOX_W).astype(BF16),
                    ob_s.reshape(DEC_BATCH, GLA_V), lw, DEC_BATCH)
        hs = _ffn(hs, lw, DEC_BATCH)
        ksr.append(k_s[:, None])
        vsr.append(v_s[:, None])
        lfs.append(lf_s[:, None])
        gs.append(s_new)

    y_prompt = hp[:, N_META:L_REAL]
    y_sample = hs[:, None, :]
    rows = lambda ts: jnp.transpose(
        jnp.stack(ts).reshape(DEPTH, BATCH, FOX_HEADS, FOX_DH, L_REAL), (0, 1, 4, 2, 3))
    return (y_prompt, y_sample, rows(kp), rows(vp), jnp.stack(lfp), jnp.stack(gp),
            jnp.stack(ksr), jnp.stack(vsr), jnp.stack(lfs), jnp.stack(gs))
```

```python
import functools

import numpy as np
import jax
import jax.numpy as jnp
from jax import lax
from jax.experimental import pallas as pl
from jax.experimental.pallas import tpu as pltpu

D_MODEL = 1024
BATCH = 2
SEQ = 8192
DEPTH = 2
DEC_BATCH = 32
PAST_LEN = 8192
PAGE_SIZE = 128
N_META = 16
FOX_HEADS = 8
FOX_DH = 64
FOX_W = FOX_HEADS * FOX_DH
GLA_HEADS = 4
GLA_DK = 64
GLA_DV = 128
GLA_K = GLA_HEADS * GLA_DK
GLA_V = GLA_HEADS * GLA_DV
GLA_RANK = 16
GLA_TAU = 16.0
D_FF = 2816
LN_EPS = 1e-5
NORM_EPS = 1e-6
NEG_INF = -1e30
ALPHA = (2.0 * DEPTH) ** 0.25
_SPLITS = (FOX_W, FOX_W, FOX_W, FOX_HEADS, GLA_K, GLA_K, GLA_V, GLA_V, GLA_RANK, D_MODEL, D_MODEL)
_OFF = np.concatenate([[0], np.cumsum(_SPLITS)]).tolist()

LANES = 128
SUBLANES = 8
VMEM_LIMIT = 56 * 1024 * 1024

L_REAL = SEQ + N_META
FOX_BLK = 256
FOX_BQ = 768
FOX_VROWS = 128
LOG2E = float(np.log2(np.e))
LP = -(-L_REAL // FOX_BLK) * FOX_BLK
PAD_ROWS = LP - L_REAL
GLA_CHUNK = 128
GLA_TILE = 768
TM_IN = 384
CUM_BLK = 128
TM_TOK = 512
N_PAGES = PAST_LEN // PAGE_SIZE
PAGES_PER_STEP = 16

F32 = jnp.float32
BF16 = jnp.bfloat16


def _dot(a, b):
    return jnp.dot(a, b, preferred_element_type=F32)


def _dot_nt(a, b):
    return lax.dot_general(a, b, (((1,), (1,)), ((), ())), preferred_element_type=F32)


def _dot_tn(a, b):
    return lax.dot_general(a, b, (((0,), (0,)), ((), ())), preferred_element_type=F32)


def _log_sigmoid(x):
    return jnp.minimum(x, 0.0) - jnp.log(1.0 + jnp.exp(-jnp.abs(x)))


def _sigmoid(x):
    return 1.0 / (1.0 + jnp.exp(-x))


def _layer_norm(y, g, b):
    mu = jnp.mean(y, axis=-1, keepdims=True)
    d = y - mu
    var = jnp.mean(d * d, axis=-1, keepdims=True)
    return d * lax.rsqrt(var + LN_EPS) * g + b


def _split3(x):
    hi = x.astype(BF16)
    r = x - hi.astype(F32)
    mid = r.astype(BF16)
    lo = (r - mid.astype(F32)).astype(BF16)
    return hi, mid, lo


def _inproj_kernel(x_ref, wqkv_ref, wg_ref, ws_ref, w2_ref, bs_ref, ba_ref, tri_ref,
                   q_ref, k_ref, v_ref, kf_ref, vf_ref, lf_ref, gq_ref, gk_ref, gv_ref, lg_ref,
                   carry_ref, *, tm):
    i = pl.program_id(1)

    @pl.when(i == 0)
    def _():
        carry_ref[...] = jnp.zeros_like(carry_ref)

    row = i * tm + lax.broadcasted_iota(jnp.int32, (tm, 1), 0)
    real = row < L_REAL
    xb = jnp.where(real, x_ref[0], 0.0).astype(BF16)

    rs = _dot(xb, ws_ref[...]) + bs_ref[...]
    lf_full = jnp.where(real, _log_sigmoid(rs), 0.0)
    lf_ref[0] = lf_full[:, :FOX_HEADS]
    z = _dot(rs.astype(BF16), w2_ref[...]) + ba_ref[...]
    lg_ref[0] = jnp.where(real, _log_sigmoid(z) * (1.0 / GLA_TAU), 0.0)

    carry = carry_ref[...]
    tri = tri_ref[...]
    cs = []
    for sb in range(tm // CUM_BLK):
        hi, mid, lo = _split3(lf_full[sb * CUM_BLK:(sb + 1) * CUM_BLK])
        c = _dot(tri, hi) + _dot(tri, mid) + _dot(tri, lo) + carry
        carry = c[CUM_BLK - 1:CUM_BLK]
        cs.append(c)
    carry_ref[...] = carry
    c = jnp.concatenate(cs, axis=0) * LOG2E

    r = _dot(xb, wqkv_ref[...])
    kf_ref[0] = r[:, FOX_W:2 * FOX_W].T
    vf_ref[0] = r[:, 2 * FOX_W:3 * FOX_W].T

    lane = lax.broadcasted_iota(jnp.int32, (1, LANES), 1)
    for h in range(FOX_HEADS):
        p, e = divmod(h, 2)
        dmask = (lane < FOX_DH) if e == 0 else (lane >= FOX_DH)
        xo = FOX_DH if e == 0 else 0
        ch = jnp.broadcast_to(c[:, h:h + 1], (tm, LANES))
        hi = ch.astype(BF16).astype(F32)
        r1 = ch - hi
        mid = r1.astype(BF16).astype(F32)
        lo = r1 - mid
        one3 = (lane >= xo + 3) & (lane < xo + 6)
        eq = jnp.where(lane == xo, hi, jnp.where(lane == xo + 1, mid, jnp.where(
            lane == xo + 2, lo, jnp.where(one3, 1.0, 0.0))))
        first3 = (lane >= xo) & (lane < xo + 3)
        ek = jnp.where(first3, 1.0, jnp.where(lane == xo + 3, -hi, jnp.where(
            lane == xo + 4, -mid, jnp.where(lane == xo + 5, -lo, 0.0))))
        ev = jnp.where(lane == xo, 1.0, 0.0)
        rq = r[:, p * LANES:(p + 1) * LANES]
        rk = r[:, FOX_W + p * LANES:FOX_W + (p + 1) * LANES]
        rv = r[:, 2 * FOX_W + p * LANES:2 * FOX_W + (p + 1) * LANES]
        q_ref[0, h] = jnp.where(dmask, rq * LOG2E, eq).astype(BF16)
        k_ref[0, h] = jnp.where(dmask, rk, ek).astype(BF16)
        v_ref[0, h] = jnp.where(dmask, rv, ev).astype(BF16)

    rg = _dot(xb, wg_ref[...])
    gq_ref[0] = rg[:, :GLA_K]
    gk_ref[0] = rg[:, GLA_K:2 * GLA_K]
    gv_ref[0] = rg[:, 2 * GLA_K:].astype(BF16)


def _const_spec(shape):
    nd = len(shape)
    return pl.BlockSpec(shape, lambda *_: (0,) * nd, pipeline_mode=pl.Buffered(1))


def _inproj(hp, lw):
    tm = TM_IN
    nt = LP // tm
    tri = jnp.asarray(np.tril(np.ones((CUM_BLK, CUM_BLK), np.float32)), BF16)
    row3 = lambda w: pl.BlockSpec((1, tm, w), lambda b, i: (b, i, 0))
    head4 = pl.BlockSpec((1, FOX_HEADS, tm, LANES), lambda b, i: (b, 0, i, 0))
    col3 = pl.BlockSpec((1, FOX_W, tm), lambda b, i: (b, 0, i))
    out_shape = (
        jax.ShapeDtypeStruct((BATCH, FOX_HEADS, LP, LANES), BF16),
        jax.ShapeDtypeStruct((BATCH, FOX_HEADS, LP, LANES), BF16),
        jax.ShapeDtypeStruct((BATCH, FOX_HEADS, LP, LANES), BF16),
        jax.ShapeDtypeStruct((BATCH, FOX_W, L_REAL), F32),
        jax.ShapeDtypeStruct((BATCH, FOX_W, L_REAL), F32),
        jax.ShapeDtypeStruct((BATCH, LP, FOX_HEADS), F32),
        jax.ShapeDtypeStruct((BATCH, LP, GLA_K), F32),
        jax.ShapeDtypeStruct((BATCH, LP, GLA_K), F32),
        jax.ShapeDtypeStruct((BATCH, LP, GLA_V), BF16),
        jax.ShapeDtypeStruct((BATCH, LP, GLA_K), F32),
    )
    return pl.pallas_call(
        functools.partial(_inproj_kernel, tm=tm),
        grid=(BATCH, nt),
        in_specs=[row3(D_MODEL), _const_spec(lw["wqkv"].shape), _const_spec(lw["wgla"].shape),
                  _const_spec(lw["ws"].shape), _const_spec(lw["w2"].shape),
                  _const_spec(lw["bs"].shape), _const_spec(lw["ba"].shape), _const_spec(tri.shape)],
        out_specs=(head4, head4, head4, col3, col3, row3(FOX_HEADS),
                   row3(GLA_K), row3(GLA_K), row3(GLA_V), row3(GLA_K)),
        out_shape=out_shape,
        scratch_shapes=[pltpu.VMEM((1, LANES), F32)],
        compiler_params=pltpu.CompilerParams(
            dimension_semantics=("parallel", "arbitrary"), vmem_limit_bytes=VMEM_LIMIT),
        name="prompt_inproj",
    )(hp, lw["wqkv"], lw["wgla"], lw["ws"], lw["w2"], lw["bs"], lw["ba"], tri)


def _fox_kernel(q_ref, k_ref, v_ref, o_ref, vt_sc, acc_sc, m_sc, mb_sc, s0_sc, s1_sc, s2_sc):
    i = pl.program_id(2)
    bq, bk = FOX_BQ, FOX_BLK
    nsub = bq // bk
    half = bk // 2

    @pl.when(i == 0)
    def _():
        for e in range(2):
            def body(c, carry):
                for s in range(2):
                    t = v_ref[0, e, pl.ds(c * bk + s * half, half), :].astype(F32)
                    tt = t.T.astype(BF16)
                    cols = slice(s * half, (s + 1) * half)
                    if e == 0:
                        vt_sc[e, c, :, cols] = tt[:FOX_VROWS]
                    else:
                        vt_sc[e, c, :FOX_DH, cols] = tt[FOX_DH:]
                        vt_sc[e, c, FOX_DH:, cols] = tt[:FOX_VROWS - FOX_DH]
                return carry
            lax.fori_loop(0, LP // bk, body, 0)

    for e in range(2):
        m_sc[e] = jnp.full((1, bq), NEG_INF, F32)
        acc_sc[e] = jnp.zeros((FOX_VROWS, bq), F32)

    sbuf = (s0_sc, s1_sc, s2_sc)
    causal = (lax.broadcasted_iota(jnp.int32, (bk, 1), 0)
              <= lax.broadcasted_iota(jnp.int32, (1, bk), 1))

    def scores(j, slot, c0):
        start = j * bk if isinstance(j, int) else pl.multiple_of(j * bk, bk)
        for e in range(2):
            st = _dot_nt(k_ref[0, e, pl.ds(start, bk), :], q_ref[0, e, c0:, :])
            sbuf[slot][e, :, c0:] = st
            mb_sc[slot, e, :, c0:] = jnp.max(st, axis=0, keepdims=True)

    def step(j, cur, c0, masked, next_c0):
        if next_c0 is not None:
            scores(j + 1, (cur + 1) % nsub, next_c0)
        for e in range(2):
            for sb in range((bq - c0) // bk):
                cols = slice(c0 + sb * bk, c0 + (sb + 1) * bk)
                st = sbuf[cur][e, :, cols]
                if masked and sb == 0:
                    st = jnp.where(causal, st, NEG_INF)
                    mblk = jnp.max(st, axis=0, keepdims=True)
                else:
                    mblk = mb_sc[cur, e, :, cols]
                m_old = m_sc[e, :, cols]
                m_new = jnp.maximum(m_old, mblk)
                alpha = jnp.exp2(m_old - m_new)
                pt = jnp.exp2(st - m_new).astype(BF16)
                acc_sc[e, :, cols] = alpha * acc_sc[e, :, cols] + _dot(vt_sc[e, j], pt)
                m_sc[e, :, cols] = m_new

    scores(0, 0, 0)

    def body(ii, carry):
        for s in range(nsub):
            step(nsub * ii + s, s, 0, False, 0)
        return carry
    lax.fori_loop(0, i, body, 0)

    for s in range(nsub):
        step(nsub * i + s, s, s * bk, True, (s + 1) * bk if s + 1 < nsub else None)

    ot = jnp.concatenate([acc_sc[e, :FOX_DH, :] / acc_sc[e, FOX_DH:FOX_DH + 1, :] for e in range(2)],
                         axis=0)
    o_ref[0] = ot.T.astype(BF16)


def _fox_prompt(q, k, v):
    nq = LP // FOX_BQ
    score_buf = pltpu.VMEM((2, FOX_BLK, FOX_BQ), F32)
    return pl.pallas_call(
        _fox_kernel,
        grid=(BATCH, FOX_HEADS // 2, nq),
        in_specs=[pl.BlockSpec((1, 2, FOX_BQ, LANES), lambda b, p, i: (b, p, i, 0)),
                  pl.BlockSpec((1, 2, LP, LANES), lambda b, p, i: (b, p, 0, 0)),
                  pl.BlockSpec((1, 2, LP, LANES), lambda b, p, i: (b, p, 0, 0))],
        out_specs=pl.BlockSpec((1, FOX_BQ, LANES), lambda b, p, i: (b, i, p)),
        out_shape=jax.ShapeDtypeStruct((BATCH, LP, FOX_W), BF16),
        scratch_shapes=[pltpu.VMEM((2, LP // FOX_BLK, FOX_VROWS, FOX_BLK), BF16),
                        pltpu.VMEM((2, FOX_VROWS, FOX_BQ), F32),
                        pltpu.VMEM((2, 1, FOX_BQ), F32),
                        pltpu.VMEM((FOX_BQ // FOX_BLK, 2, 1, FOX_BQ), F32),
                        score_buf, score_buf, score_buf],
        compiler_params=pltpu.CompilerParams(
            dimension_semantics=("parallel", "parallel", "arbitrary"), vmem_limit_bytes=VMEM_LIMIT),
        name="fox_prompt",
    )(q, k, v)


def _gla_tables():
    c = GLA_CHUNK
    t = np.arange(c)[:, None]
    j = np.arange(c)[None, :]
    mats = [(j <= t).astype(np.float32), (j > t).astype(np.float32)]
    masks = [np.eye(c, dtype=np.float32)]
    blk = c
    while blk >= 2:
        half = blk // 2
        mid = (t // blk) * blk + half
        mats.append((j <= t).astype(np.float32) - (j <= mid).astype(np.float32))
        s = j
        masks.append((((t // blk) == (s // blk)) & ((t % blk) >= half) & ((s % blk) < half))
                     .astype(np.float32))
        blk = half
    return np.concatenate(mats, axis=0), np.stack([np.concatenate([m, m], axis=0) for m in masks])


_GLA_LEVELS = 7


def _gla_kernel(q_ref, k_ref, lg_ref, v_ref, dall_ref, masks_ref, o_ref, sfin_ref, st_sc):
    i = pl.program_id(2)
    c = GLA_CHUNK

    @pl.when(i == 0)
    def _():
        st_sc[...] = jnp.zeros_like(st_sc)

    lane = lax.broadcasted_iota(jnp.int32, (1, LANES), 1)
    hmask = (lane < GLA_DK, lane >= GLA_DK)

    for ci in range(GLA_TILE // c):
        rows = slice(ci * c, (ci + 1) * c)
        q = q_ref[0, rows, :]
        k = k_ref[0, rows, :]
        lg = lg_ref[0, rows, :]
        hi = lg.astype(BF16)
        lo = (lg - hi.astype(F32)).astype(BF16)
        e2 = _dot(dall_ref[...], jnp.concatenate([hi, lo], axis=1))
        ex = e2[:, :LANES] + e2[:, LANES:]
        bc = ex[0:c]
        q_in = q * jnp.exp(bc)
        k_dec = k * jnp.exp(ex[c:2 * c])
        qs = [q]
        ks = [k.astype(BF16)]
        for lv in range(_GLA_LEVELS):
            f = jnp.exp(-jnp.abs(ex[(2 + lv) * c:(3 + lv) * c]))
            qs.append(q * f)
            ks.append((k * f).astype(BF16))
        decay_all = jnp.exp(bc[c - 1:c])
        a2 = jnp.zeros((2 * c, c), F32)
        for lv in range(_GLA_LEVELS + 1):
            ql = jnp.concatenate([jnp.where(hmask[0], qs[lv], 0.0),
                                  jnp.where(hmask[1], qs[lv], 0.0)], axis=0).astype(BF16)
            a2 = a2 + masks_ref[lv] * _dot_nt(ql, ks[lv])
        for h in range(2):
            a = a2[h * c:(h + 1) * c]
            vh = v_ref[0, rows, h * GLA_DV:(h + 1) * GLA_DV]
            st = st_sc[h]
            o = _dot(a.astype(BF16), vh) + _dot_nt(jnp.where(hmask[h], q_in, 0.0).astype(BF16),
                                                   st.astype(BF16))
            o_ref[0, rows, h * GLA_DV:(h + 1) * GLA_DV] = o
            kd = jnp.where(hmask[h], k_dec, 0.0).astype(BF16)
            st_sc[h] = decay_all * st + _dot_tn(vh, kd)

    @pl.when(i == pl.num_programs(2) - 1)
    def _():
        for h in range(2):
            s = st_sc[h].T
            sfin_ref[0, h] = s[h * GLA_DK:(h + 1) * GLA_DK, :]


def _gla_prompt(gq, gk, lg, gv):
    dall_np, masks_np = _gla_tables()
    dall = jnp.asarray(dall_np, BF16)
    masks = jnp.asarray(masks_np, F32)
    t = GLA_TILE
    pair = pl.BlockSpec((1, t, LANES), lambda b, p, i: (b, i, p))
    wide = pl.BlockSpec((1, t, 2 * GLA_DV), lambda b, p, i: (b, i, p))
    return pl.pallas_call(
        _gla_kernel,
        grid=(BATCH, GLA_HEADS // 2, LP // t),
        in_specs=[pair, pair, pair, wide, _const_spec(dall.shape), _const_spec(masks.shape)],
        out_specs=(wide, pl.BlockSpec((1, 2, GLA_DK, GLA_DV), lambda b, p, i: (b, p, 0, 0))),
        out_shape=(jax.ShapeDtypeStruct((BATCH, LP, GLA_V), F32),
                   jax.ShapeDtypeStruct((BATCH, GLA_HEADS, GLA_DK, GLA_DV), F32)),
        scratch_shapes=[pltpu.VMEM((2, GLA_DV, LANES), F32)],
        compiler_params=pltpu.CompilerParams(
            dimension_semantics=("parallel", "parallel", "arbitrary"), vmem_limit_bytes=VMEM_LIMIT),
        name="gla_prompt",
    )(gq, gk, lg, gv, dall, masks)


def _merge_kernel(h_ref, oa_ref, ob_ref, wg_ref, wpa_ref, wpb_ref, wo_ref, gn_ref, g_ref, b_ref,
                  out_ref):
    x = h_ref[...]
    xb = x.astype(BF16)
    r = _dot(xb, wg_ref[...])
    rb = r[:, :GLA_V]
    ga = r[:, GLA_V:GLA_V + D_MODEL]
    gb = r[:, GLA_V + D_MODEL:]
    ob = ob_ref[...]
    parts = []
    for hd in range(GLA_HEADS):
        o = ob[:, hd * GLA_DV:(hd + 1) * GLA_DV]
        ms = jnp.mean(o * o, axis=-1, keepdims=True)
        parts.append(o * lax.rsqrt(ms + NORM_EPS) * gn_ref[...])
    obn = jnp.concatenate(parts, axis=1) * (rb * _sigmoid(rb))
    y_a = _dot(oa_ref[...], wpa_ref[...])
    y_b = _dot(obn.astype(BF16), wpb_ref[...])
    mixed = _sigmoid(ga) * y_a + _sigmoid(gb) * y_b
    y = ALPHA * x + _dot(mixed.astype(BF16), wo_ref[...])
    out_ref[...] = _layer_norm(y, g_ref[...], b_ref[...])


def _merge(h, oa, ob, lw, tm):
    m = h.shape[0]
    row = lambda w: pl.BlockSpec((tm, w), lambda i: (i, 0))
    ws = (lw["wmg"], lw["wpa"], lw["wpb"], lw["wo"], lw["gn"], lw["ln1g"], lw["ln1b"])
    return pl.pallas_call(
        _merge_kernel,
        grid=(m // tm,),
        in_specs=[row(D_MODEL), row(FOX_W), row(GLA_V)] + [_const_spec(w.shape) for w in ws],
        out_specs=row(D_MODEL),
        out_shape=jax.ShapeDtypeStruct((m, D_MODEL), F32),
        compiler_params=pltpu.CompilerParams(
            dimension_semantics=("parallel",), vmem_limit_bytes=VMEM_LIMIT),
        name="merge",
    )(h, oa, ob, *ws)


def _ffn_kernel(h_ref, wgate_ref, wup_ref, wdown_ref, g_ref, b_ref, out_ref):
    x = h_ref[...]
    xb = x.astype(BF16)
    gt = _dot(xb, wgate_ref[...])
    up = _dot(xb, wup_ref[...])
    hdn = (gt * _sigmoid(gt) * up).astype(BF16)
    y = ALPHA * x + _dot(hdn, wdown_ref[...])
    out_ref[...] = _layer_norm(y, g_ref[...], b_ref[...])


def _ffn(h, lw, tm):
    m = h.shape[0]
    row = pl.BlockSpec((tm, D_MODEL), lambda i: (i, 0))
    ws = (lw["wgate"], lw["wup"], lw["wdown"], lw["ln2g"], lw["ln2b"])
    return pl.pallas_call(
        _ffn_kernel,
        grid=(m // tm,),
        in_specs=[row] + [_const_spec(w.shape) for w in ws],
        out_specs=row,
        out_shape=jax.ShapeDtypeStruct((m, D_MODEL), F32),
        compiler_params=pltpu.CompilerParams(
            dimension_semantics=("parallel",), vmem_limit_bytes=VMEM_LIMIT),
        name="ffn",
    )(h, *ws)


def _sample_inproj_kernel(x_ref, wqkv_ref, wg_ref, ws_ref, w2_ref, bs_ref, ba_ref,
                          r1_ref, r2_ref, lf_ref, lg_ref):
    xb = x_ref[...].astype(BF16)
    r1_ref[...] = _dot(xb, wqkv_ref[...])
    r2_ref[...] = _dot(xb, wg_ref[...])
    rs = _dot(xb, ws_ref[...]) + bs_ref[...]
    lf_ref[...] = _log_sigmoid(rs)
    z = _dot(rs.astype(BF16), w2_ref[...]) + ba_ref[...]
    lg_ref[...] = _log_sigmoid(z) * (1.0 / GLA_TAU)


def _sample_inproj(x, lw):
    ws = (lw["wqkv"], lw["wgla"], lw["ws"], lw["w2"], lw["bs"], lw["ba"])
    full = lambda shape: pl.BlockSpec(shape, lambda i: (0,) * len(shape))
    out_shape = (jax.ShapeDtypeStruct((DEC_BATCH, 3 * FOX_W), F32),
                 jax.ShapeDtypeStruct((DEC_BATCH, 2 * GLA_K + GLA_V), F32),
                 jax.ShapeDtypeStruct((DEC_BATCH, LANES), F32),
                 jax.ShapeDtypeStruct((DEC_BATCH, GLA_K), F32))
    return pl.pallas_call(
        _sample_inproj_kernel,
        grid=(1,),
        in_specs=[full(x.shape)] + [full(w.shape) for w in ws],
        out_specs=tuple(full(s.shape) for s in out_shape),
        out_shape=out_shape,
        compiler_params=pltpu.CompilerParams(vmem_limit_bytes=VMEM_LIMIT),
        name="sample_inproj",
    )(x, *ws)


def _decode_tables():
    j = np.arange(LANES)
    ut = np.concatenate([j[:, None] > j[None, :], np.ones((LANES, LANES), bool)], axis=1)
    nr = PAGES_PER_STEP * FOX_HEADS
    r = np.arange(nr)
    same = (r[:, None] % FOX_HEADS) == (r[None, :] % FOX_HEADS)
    us = np.zeros((nr + 2 * SUBLANES, nr), bool)
    us[:nr] = same & (r[None, :] > r[:, None])
    us[nr:nr + FOX_HEADS] = (r[None, :] % FOX_HEADS) == np.arange(FOX_HEADS)[:, None]
    return ut.astype(np.float32), us.astype(np.float32)


def _fox_decode_kernel(pt_ref, qbd_ref, knb_ref, vrow_ref, lfn_ref, ut_ref, us_ref, *rest):
    npg = PAGES_PER_STEP
    kp = rest[0:npg]
    vp = rest[npg:2 * npg]
    lft_ref = rest[2 * npg]
    o_ref = rest[2 * npg + 1]
    m_sc, l_sc, acc_sc, carry_sc = rest[2 * npg + 2:]
    b = pl.program_id(0)
    j = pl.program_id(1)
    first_page = (pl.num_programs(1) - 1 - j) * npg
    nh = FOX_HEADS
    nr = npg * nh
    qbd = qbd_ref[0]

    def page2d(ref):
        return ref[...].reshape(FOX_W, PAGE_SIZE).astype(BF16)

    @pl.when(j == 0)
    def _():
        m_sc[...] = _dot(qbd, page2d(knb_ref.at[0]))
        lane = lax.broadcasted_iota(jnp.int32, (nh, LANES), 1)
        l_sc[...] = jnp.where(lane == 0, 1.0, 0.0)
        acc_sc[...] = jnp.broadcast_to(vrow_ref[0].astype(BF16).astype(F32), (nh, FOX_W))
        carry_sc[...] = lfn_ref[0]

    lfc = jnp.concatenate([lft_ref[pt_ref[b, first_page + g]] for g in range(npg)],
                          axis=0)
    w = _dot(jnp.concatenate(_split3(lfc), axis=0), ut_ref[...])

    s = [_dot(qbd, page2d(kp[g])) for g in range(npg)]

    wsum = w[0:nr] + w[nr:2 * nr] + w[2 * nr:3 * nr]
    x = _dot(us_ref[...], jnp.concatenate(_split3(wsum[:, LANES:]), axis=1))
    xs = x[:, :LANES] + x[:, LANES:2 * LANES] + x[:, 2 * LANES:]
    carry = carry_sc[...]
    sb = [s[g] + (wsum[g * nh:(g + 1) * nh, :LANES] + xs[g * nh:(g + 1) * nh] + carry)
          for g in range(npg)]
    carry_sc[...] = carry + xs[nr:nr + nh]

    mx = sb[0]
    for g in range(1, npg):
        mx = jnp.maximum(mx, sb[g])
    m_old = m_sc[...]
    m_new = jnp.maximum(m_old, jnp.max(mx, axis=1, keepdims=True))
    alpha = jnp.exp(m_old - m_new)
    p = [jnp.exp(sb[g] - m_new) for g in range(npg)]
    psum = p[0]
    for g in range(1, npg):
        psum = psum + p[g]
    l_sc[...] = alpha * l_sc[...] + psum
    m_sc[...] = m_new
    pv = _dot_nt(p[0].astype(BF16), page2d(vp[0]))
    for g in range(1, npg):
        pv = pv + _dot_nt(p[g].astype(BF16), page2d(vp[g]))
    acc_sc[...] = jnp.concatenate([alpha] * (FOX_W // LANES), axis=1) * acc_sc[...] + pv

    @pl.when(j == pl.num_programs(1) - 1)
    def _():
        ltot = jnp.sum(l_sc[...], axis=1, keepdims=True)
        own = (lax.broadcasted_iota(jnp.int32, (nh, FOX_W), 1) // FOX_DH
               == lax.broadcasted_iota(jnp.int32, (nh, FOX_W), 0))
        o_ref[0] = jnp.sum(jnp.where(own, acc_sc[...] / ltot, 0.0), axis=0, keepdims=True)


def _fox_decode(layer, page_table, qbd, knb, vrow, lfn, cache_kt, cache_vt, cache_lft):
    npg = PAGES_PER_STEP
    nsteps = N_PAGES // npg
    ut_np, us_np = _decode_tables()
    ut = jnp.asarray(ut_np, BF16)
    us = jnp.asarray(us_np, BF16)

    def page_map(g, tail):
        def f(b, j, pt):
            return (layer, pt[b, (nsteps - 1 - j) * npg + g]) + tail
        return f

    kv_specs = [pl.BlockSpec((None, None, FOX_HEADS, FOX_DH, PAGE_SIZE), page_map(g, (0, 0, 0)))
                for g in range(npg)]
    lf_spec = pl.BlockSpec((None,) + cache_lft.shape[1:], lambda b, j, pt: (layer, 0, 0, 0),
                           pipeline_mode=pl.Buffered(1))
    per_b = lambda shape: pl.BlockSpec((1,) + shape, lambda b, j, pt: (b,) + (0,) * len(shape))
    const = lambda shape: pl.BlockSpec(shape, lambda b, j, pt: (0,) * len(shape))
    hdl = (FOX_HEADS, FOX_DH, LANES)
    grid_spec = pltpu.PrefetchScalarGridSpec(
        num_scalar_prefetch=1,
        grid=(DEC_BATCH, nsteps),
        in_specs=[per_b((FOX_HEADS, FOX_W)), per_b(hdl), per_b((1, FOX_W)),
                  per_b((FOX_HEADS, LANES)), const(ut.shape), const(us.shape)]
        + kv_specs + kv_specs + [lf_spec],
        out_specs=per_b((1, FOX_W)),
        scratch_shapes=[pltpu.VMEM((FOX_HEADS, LANES), F32), pltpu.VMEM((FOX_HEADS, LANES), F32),
                        pltpu.VMEM((FOX_HEADS, FOX_W), F32), pltpu.VMEM((FOX_HEADS, LANES), F32)],
    )
    return pl.pallas_call(
        _fox_decode_kernel,
        grid_spec=grid_spec,
        out_shape=jax.ShapeDtypeStruct((DEC_BATCH, 1, FOX_W), F32),
        compiler_params=pltpu.CompilerParams(
            dimension_semantics=("parallel", "arbitrary"), vmem_limit_bytes=VMEM_LIMIT),
        name="fox_decode",
    )(page_table, qbd, knb, vrow, lfn, ut, us,
      *([cache_kt] * npg), *([cache_vt] * npg), cache_lft)


def _gla_decode_kernel(q_ref, k_ref, g_ref, v_ref, s_ref, o_ref, sn_ref):
    eye = (lax.broadcasted_iota(jnp.int32, (GLA_DK, GLA_DK), 0)
           == lax.broadcasted_iota(jnp.int32, (GLA_DK, GLA_DK), 1))

    def col(r):
        return jnp.sum(jnp.where(eye, jnp.broadcast_to(r, (GLA_DK, GLA_DK)), 0.0),
                       axis=1, keepdims=True)

    for h in range(GLA_HEADS):
        qc = col(q_ref[0, h:h + 1, :])
        kc = col(k_ref[0, h:h + 1, :])
        ac = col(jnp.exp(g_ref[0, h:h + 1, :]))
        sn = ac * s_ref[0, h] + kc * v_ref[0, h:h + 1, :]
        sn_ref[0, h] = sn
        o_ref[0, h:h + 1, :] = jnp.sum(qc * sn, axis=0, keepdims=True)


def _gla_decode(gq, gk, lg, gv, state):
    hk = pl.BlockSpec((1, GLA_HEADS, GLA_DK), lambda b: (b, 0, 0))
    hv = pl.BlockSpec((1, GLA_HEADS, GLA_DV), lambda b: (b, 0, 0))
    st = pl.BlockSpec((1, GLA_HEADS, GLA_DK, GLA_DV), lambda b: (b, 0, 0, 0))
    return pl.pallas_call(
        _gla_decode_kernel,
        grid=(DEC_BATCH,),
        in_specs=[hk, hk, hk, hv, st],
        out_specs=(hv, st),
        out_shape=(jax.ShapeDtypeStruct((DEC_BATCH, GLA_HEADS, GLA_DV), F32),
                   jax.ShapeDtypeStruct((DEC_BATCH, GLA_HEADS, GLA_DK, GLA_DV), F32)),
        compiler_params=pltpu.CompilerParams(
            dimension_semantics=("parallel",), vmem_limit_bytes=VMEM_LIMIT),
        name="gla_decode",
    )(gq, gk, lg, gv, state)


def _layer_weights(l, w_in, b_f, w_a2, b_a, gla_norm_g, w_pa, w_pb, w_o, ln1_g, ln1_b,
                   w_gate, w_up, w_down, ln2_g, ln2_b):
    w = w_in[l]
    o = _OFF
    sc_f = FOX_DH ** -0.5
    sc_g = GLA_DK ** -0.5
    seg = lambda a: w[:, o[a]:o[a + 1]]
    wqkv = jnp.concatenate([seg(0) * sc_f, seg(1), seg(2)], axis=1).astype(BF16)
    wgla = jnp.concatenate([seg(4) * sc_g, seg(5), seg(6)], axis=1).astype(BF16)
    ws = jnp.zeros((D_MODEL, LANES), F32)
    ws = ws.at[:, :FOX_HEADS].set(seg(3)).at[:, FOX_HEADS:FOX_HEADS + GLA_RANK].set(seg(8))
    w2 = jnp.zeros((LANES, GLA_K), F32).at[FOX_HEADS:FOX_HEADS + GLA_RANK].set(w_a2[l])
    bs = jnp.zeros((1, LANES), F32).at[0, :FOX_HEADS].set(b_f[l])
    wmg = jnp.concatenate([seg(7), seg(9), seg(10)], axis=1).astype(BF16)
    return dict(
        wqkv=wqkv, wgla=wgla, ws=ws.astype(BF16), w2=w2.astype(BF16), bs=bs, ba=b_a[l][None],
        wmg=wmg, wpa=w_pa[l].astype(BF16), wpb=w_pb[l].astype(BF16), wo=w_o[l].astype(BF16),
        gn=gla_norm_g[l][None], ln1g=ln1_g[l][None], ln1b=ln1_b[l][None],
        wgate=w_gate[l].astype(BF16), wup=w_up[l].astype(BF16), wdown=w_down[l].astype(BF16),
        ln2g=ln2_g[l][None], ln2b=ln2_b[l][None])


def kernel(x_prompt, x_sample, cache_k, cache_v, cache_lf, state_gla, page_table, meta, w_in, b_f,
           w_a2, b_a, gla_norm_g, w_pa, w_pb, w_o, ln1_g, ln1_b, w_gate, w_up, w_down, ln2_g, ln2_b):
    assert x_prompt.shape == (BATCH, SEQ, D_MODEL) and x_sample.shape == (DEC_BATCH, 1, D_MODEL)
    hp = jnp.concatenate([jnp.broadcast_to(meta.astype(F32)[None], (BATCH, N_META, D_MODEL)), x_prompt,
                          jnp.zeros((BATCH, PAD_ROWS, D_MODEL), F32)], axis=1)
    hs = x_sample.reshape(DEC_BATCH, D_MODEL)
    cache_kt = jnp.transpose(cache_k, (0, 1, 3, 4, 2))
    cache_vt = jnp.transpose(cache_v, (0, 1, 3, 4, 2))
    cache_lft = jnp.transpose(cache_lf, (0, 1, 3, 2))

    kp, vp, lfp, gp, ksr, vsr, lfs, gs = [], [], [], [], [], [], [], []
    for l in range(DEPTH):
        lw = _layer_weights(l, w_in, b_f, w_a2, b_a, gla_norm_g, w_pa, w_pb, w_o, ln1_g, ln1_b,
                            w_gate, w_up, w_down, ln2_g, ln2_b)
        qa, ka, va, kf, vf, lf, gq, gk, gv, lg = _inproj(hp, lw)
        oa = _fox_prompt(qa, ka, va)
        ob, sfin = _gla_prompt(gq, gk, lg, gv)
        h2 = hp.reshape(BATCH * LP, D_MODEL)
        h2 = _merge(h2, oa.reshape(BATCH * LP, FOX_W), ob.reshape(BATCH * LP, GLA_V), lw, TM_TOK)
        h2 = _ffn(h2, lw, TM_TOK)
        hp = h2.reshape(BATCH, LP, D_MODEL)
        kp.append(kf)
        vp.append(vf)
        lfp.append(lf[:, :L_REAL])
        gp.append(sfin)
        r1, r2, lfs_full, lgs = _sample_inproj(hs, lw)
        q_s = r1[:, :FOX_W].reshape(DEC_BATCH, FOX_HEADS, FOX_DH)
        k_s = r1[:, FOX_W:2 * FOX_W].reshape(DEC_BATCH, FOX_HEADS, FOX_DH)
        v_s = r1[:, 2 * FOX_W:].reshape(DEC_BATCH, FOX_HEADS, FOX_DH)
        lf_s = lfs_full[:, :FOX_HEADS]
        lanes = lambda t: jnp.broadcast_to(t[..., None], t.shape + (LANES,))
        eye = jnp.eye(FOX_HEADS, dtype=F32)
        qbd = (q_s[:, :, None, :] * eye[None, :, :, None]).reshape(DEC_BATCH, FOX_HEADS, FOX_W)
        oa_s = _fox_decode(l, page_table, qbd.astype(BF16), lanes(k_s), r1[:, None, 2 * FOX_W:],
                           lanes(lf_s), cache_kt, cache_vt, cache_lft)
        ob_s, s_new = _gla_decode(r2[:, :GLA_K].reshape(DEC_BATCH, GLA_HEADS, GLA_DK),
                                  r2[:, GLA_K:2 * GLA_K].reshape(DEC_BATCH, GLA_HEADS, GLA_DK),
                                  lgs.reshape(DEC_BATCH, GLA_HEADS, GLA_DK),
                                  r2[:, 2 * GLA_K:].reshape(DEC_BATCH, GLA_HEADS, GLA_DV),
                                  state_gla[l])
        hs = _merge(hs, oa_s.reshape(DEC_BATCH, FOX_W).astype(BF16),
                    ob_s.reshape(DEC_BATCH, GLA_V), lw, DEC_BATCH)
        hs = _ffn(hs, lw, DEC_BATCH)
        ksr.append(k_s[:, None])
        vsr.append(v_s[:, None])
        lfs.append(lf_s[:, None])
        gs.append(s_new)

    y_prompt = hp[:, N_META:L_REAL]
    y_sample = hs[:, None, :]
    rows = lambda ts: jnp.transpose(
        jnp.stack(ts).reshape(DEPTH, BATCH, FOX_HEADS, FOX_DH, L_REAL), (0, 1, 4, 2, 3))
    return (y_prompt, y_sample, rows(kp), rows(vp), jnp.stack(lfp), jnp.stack(gp),
            jnp.stack(ksr), jnp.stack(vsr), jnp.stack(lfs), jnp.stack(gs))
```

```python
import functools

import numpy as np
import jax
import jax.numpy as jnp
from jax import lax
from jax.experimental import pallas as pl
from jax.experimental.pallas import tpu as pltpu

D_MODEL = 1024
BATCH = 2
SEQ = 8192
DEPTH = 2
DEC_BATCH = 32
PAST_LEN = 8192
PAGE_SIZE = 128
N_META = 16
FOX_HEADS = 8
FOX_DH = 64
FOX_W = FOX_HEADS * FOX_DH
GLA_HEADS = 4
GLA_DK = 64
GLA_DV = 128
GLA_K = GLA_HEADS * GLA_DK
GLA_V = GLA_HEADS * GLA_DV
GLA_RANK = 16
GLA_TAU = 16.0
D_FF = 2816
LN_EPS = 1e-5
NORM_EPS = 1e-6
NEG_INF = -1e30
ALPHA = (2.0 * DEPTH) ** 0.25
_SPLITS = (FOX_W, FOX_W, FOX_W, FOX_HEADS, GLA_K, GLA_K, GLA_V, GLA_V, GLA_RANK, D_MODEL, D_MODEL)
_OFF = np.concatenate([[0], np.cumsum(_SPLITS)]).tolist()

LANES = 128
SUBLANES = 8
VMEM_LIMIT = 56 * 1024 * 1024

L_REAL = SEQ + N_META
FOX_BLK = 256
FOX_BQ = 768
FOX_VROWS = 128
LOG2E = float(np.log2(np.e))
LP = -(-L_REAL // FOX_BLK) * FOX_BLK
PAD_ROWS = LP - L_REAL
GLA_CHUNK = 128
GLA_TILE = 768
TM_IN = 384
CUM_BLK = 128
TM_TOK = 512
N_PAGES = PAST_LEN // PAGE_SIZE
PAGES_PER_STEP = 16
GLA_DEC_ROWS = 8

F32 = jnp.float32
BF16 = jnp.bfloat16


def _dot(a, b):
    return jnp.dot(a, b, preferred_element_type=F32)


def _dot_nt(a, b):
    return lax.dot_general(a, b, (((1,), (1,)), ((), ())), preferred_element_type=F32)


def _dot_tn(a, b):
    return lax.dot_general(a, b, (((0,), (0,)), ((), ())), preferred_element_type=F32)


def _log_sigmoid(x):
    return jnp.minimum(x, 0.0) - jnp.log(1.0 + jnp.exp(-jnp.abs(x)))


def _sigmoid(x):
    return 1.0 / (1.0 + jnp.exp(-x))


def _layer_norm(y, g, b):
    mu = jnp.mean(y, axis=-1, keepdims=True)
    d = y - mu
    var = jnp.mean(d * d, axis=-1, keepdims=True)
    return d * lax.rsqrt(var + LN_EPS) * g + b


def _split3(x):
    hi = x.astype(BF16)
    r = x - hi.astype(F32)
    mid = r.astype(BF16)
    lo = (r - mid.astype(F32)).astype(BF16)
    return hi, mid, lo


def _fox_place_table():
    nh = FOX_HEADS
    t = np.zeros((LANES, 2 * nh * LANES), np.float32)
    for h in range(nh):
        xo = FOX_DH if h % 2 == 0 else 0
        qc = h * LANES + xo
        kc = (nh + h) * LANES + xo
        for part in range(3):
            t[part * nh + h, qc + part] = 1.0
            t[3 * nh, qc + 3 + part] = 1.0
            t[3 * nh, kc + part] = 1.0
            t[part * nh + h, kc + 3 + part] = -1.0
    return t


def _inproj_kernel(x_ref, wqkv_ref, wg_ref, ws_ref, w2_ref, bs_ref, ba_ref, tri_ref, place_ref,
                   q_ref, k_ref, v_ref, kf_ref, vf_ref, lf_ref, gq_ref, gk_ref, gv_ref, lg_ref,
                   carry_ref, *, tm):
    i = pl.program_id(1)

    @pl.when(i == 0)
    def _():
        carry_ref[...] = jnp.zeros_like(carry_ref)

    row = i * tm + lax.broadcasted_iota(jnp.int32, (tm, 1), 0)
    real = row < L_REAL
    xb = jnp.where(real, x_ref[0], 0.0).astype(BF16)

    rs = _dot(xb, ws_ref[...]) + bs_ref[...]
    lf_full = jnp.where(real, _log_sigmoid(rs), 0.0)
    lf_ref[0] = lf_full[:, :FOX_HEADS]
    z = _dot(rs.astype(BF16), w2_ref[...]) + ba_ref[...]
    lg_ref[0] = jnp.where(real, _log_sigmoid(z) * (1.0 / GLA_TAU), 0.0)

    carry = carry_ref[...]
    tri = tri_ref[...]
    cs = []
    for sb in range(tm // CUM_BLK):
        hi, mid, lo = _split3(lf_full[sb * CUM_BLK:(sb + 1) * CUM_BLK])
        c = _dot(tri, hi) + _dot(tri, mid) + _dot(tri, lo) + carry
        carry = c[CUM_BLK - 1:CUM_BLK]
        cs.append(c)
    carry_ref[...] = carry
    c = jnp.concatenate(cs, axis=0) * LOG2E

    r = _dot(xb, wqkv_ref[...])
    kf_ref[0] = r[:, FOX_W:2 * FOX_W].T
    vf_ref[0] = r[:, 2 * FOX_W:3 * FOX_W].T

    lane = lax.broadcasted_iota(jnp.int32, (1, LANES), 1)
    hi = c.astype(BF16).astype(F32)
    r1 = c - hi
    mid = r1.astype(BF16).astype(F32)
    lo = r1 - mid
    nh = FOX_HEADS
    parts = jnp.where(lane < nh, hi, jnp.where(lane < 2 * nh, pltpu.roll(mid, nh, axis=1), jnp.where(
        lane < 3 * nh, pltpu.roll(lo, 2 * nh, axis=1), jnp.where(lane == 3 * nh, 1.0, 0.0))))
    ext = _dot(parts.astype(BF16), place_ref[...])
    for h in range(FOX_HEADS):
        p, e = divmod(h, 2)
        dmask = (lane < FOX_DH) if e == 0 else (lane >= FOX_DH)
        xo = FOX_DH if e == 0 else 0
        ev = jnp.where(lane == xo, 1.0, 0.0)
        rq = r[:, p * LANES:(p + 1) * LANES]
        rk = r[:, FOX_W + p * LANES:FOX_W + (p + 1) * LANES]
        rv = r[:, 2 * FOX_W + p * LANES:2 * FOX_W + (p + 1) * LANES]
        q_ref[0, h] = jnp.where(dmask, rq * LOG2E, ext[:, h * LANES:(h + 1) * LANES]).astype(BF16)
        k_ref[0, h] = jnp.where(dmask, rk, ext[:, (nh + h) * LANES:(nh + h + 1) * LANES]).astype(BF16)
        v_ref[0, h] = jnp.where(dmask, rv, ev).astype(BF16)

    rg = _dot(xb, wg_ref[...])
    gq_ref[0] = rg[:, :GLA_K]
    gk_ref[0] = rg[:, GLA_K:2 * GLA_K]
    gv_ref[0] = rg[:, 2 * GLA_K:].astype(BF16)


def _const_spec(shape):
    nd = len(shape)
    return pl.BlockSpec(shape, lambda *_: (0,) * nd, pipeline_mode=pl.Buffered(1))


def _layer_spec(w, l):
    nd = w.ndim - 1
    return pl.BlockSpec((None,) + w.shape[1:], lambda *_: (l,) + (0,) * nd,
                        pipeline_mode=pl.Buffered(1))


def _inproj(hp, lw, l):
    tm = TM_IN
    nt = LP // tm
    tri = jnp.asarray(np.tril(np.ones((CUM_BLK, CUM_BLK), np.float32)), BF16)
    place = jnp.asarray(_fox_place_table(), BF16)
    row3 = lambda w: pl.BlockSpec((1, tm, w), lambda b, i: (b, i, 0))
    head4 = pl.BlockSpec((1, FOX_HEADS, tm, LANES), lambda b, i: (b, 0, i, 0))
    col3 = pl.BlockSpec((1, FOX_W, tm), lambda b, i: (b, 0, i))
    out_shape = (
        jax.ShapeDtypeStruct((BATCH, FOX_HEADS, LP, LANES), BF16),
        jax.ShapeDtypeStruct((BATCH, FOX_HEADS, LP, LANES), BF16),
        jax.ShapeDtypeStruct((BATCH, FOX_HEADS, LP, LANES), BF16),
        jax.ShapeDtypeStruct((BATCH, FOX_W, L_REAL), F32),
        jax.ShapeDtypeStruct((BATCH, FOX_W, L_REAL), F32),
        jax.ShapeDtypeStruct((BATCH, LP, FOX_HEADS), F32),
        jax.ShapeDtypeStruct((BATCH, LP, GLA_K), F32),
        jax.ShapeDtypeStruct((BATCH, LP, GLA_K), F32),
        jax.ShapeDtypeStruct((BATCH, LP, GLA_V), BF16),
        jax.ShapeDtypeStruct((BATCH, LP, GLA_K), F32),
    )
    return pl.pallas_call(
        functools.partial(_inproj_kernel, tm=tm),
        grid=(BATCH, nt),
        in_specs=[row3(D_MODEL)]
        + [_layer_spec(lw[n], l) for n in ("wqkv", "wgla", "ws", "w2", "bs", "ba")]
        + [_const_spec(tri.shape), _const_spec(place.shape)],
        out_specs=(head4, head4, head4, col3, col3, row3(FOX_HEADS),
                   row3(GLA_K), row3(GLA_K), row3(GLA_V), row3(GLA_K)),
        out_shape=out_shape,
        scratch_shapes=[pltpu.VMEM((1, LANES), F32)],
        compiler_params=pltpu.CompilerParams(
            dimension_semantics=("parallel", "arbitrary"), vmem_limit_bytes=VMEM_LIMIT),
        name="prompt_inproj",
    )(hp, lw["wqkv"], lw["wgla"], lw["ws"], lw["w2"], lw["bs"], lw["ba"], tri, place)


def _fox_kernel(q_ref, k_ref, v_ref, o_ref, vt_sc, acc_sc, m_sc, mb_sc, s0_sc, s1_sc, s2_sc):
    i = pl.program_id(2)
    bq, bk = FOX_BQ, FOX_BLK
    nsub = bq // bk
    half = bk // 2

    @pl.when(i == 0)
    def _():
        for e in range(2):
            def body(c, carry):
                for s in range(2):
                    t = v_ref[0, e, pl.ds(c * bk + s * half, half), :].astype(F32)
                    tt = t.T.astype(BF16)
                    cols = slice(s * half, (s + 1) * half)
                    if e == 0:
                        vt_sc[e, c, :, cols] = tt[:FOX_VROWS]
                    else:
                        vt_sc[e, c, :FOX_DH, cols] = tt[FOX_DH:]
                        vt_sc[e, c, FOX_DH:, cols] = tt[:FOX_VROWS - FOX_DH]
                return carry
            lax.fori_loop(0, LP // bk, body, 0)

    for e in range(2):
        m_sc[e] = jnp.full((1, bq), NEG_INF, F32)
        acc_sc[e] = jnp.zeros((FOX_VROWS, bq), F32)

    sbuf = (s0_sc, s1_sc, s2_sc)
    causal = (lax.broadcasted_iota(jnp.int32, (bk, 1), 0)
              <= lax.broadcasted_iota(jnp.int32, (1, bk), 1))

    def scores(j, slot, c0):
        start = j * bk if isinstance(j, int) else pl.multiple_of(j * bk, bk)
        for e in range(2):
            st = _dot_nt(k_ref[0, e, pl.ds(start, bk), :], q_ref[0, e, c0:, :])
            sbuf[slot][e, :, c0:] = st
            mb_sc[slot, e, :, c0:] = jnp.max(st, axis=0, keepdims=True)

    def step(j, cur, c0, masked, next_c0):
        if next_c0 is not None:
            scores(j + 1, (cur + 1) % nsub, next_c0)
        for e in range(2):
            for sb in range((bq - c0) // bk):
                cols = slice(c0 + sb * bk, c0 + (sb + 1) * bk)
                st = sbuf[cur][e, :, cols]
                if masked and sb == 0:
                    st = jnp.where(causal, st, NEG_INF)
                    mblk = jnp.max(st, axis=0, keepdims=True)
                else:
                    mblk = mb_sc[cur, e, :, cols]
                m_old = m_sc[e, :, cols]
                m_new = jnp.maximum(m_old, mblk)
                alpha = jnp.exp2(m_old - m_new)
                pt = jnp.exp2(st - m_new).astype(BF16)
                acc_sc[e, :, cols] = alpha * acc_sc[e, :, cols] + _dot(vt_sc[e, j], pt)
                m_sc[e, :, cols] = m_new

    scores(0, 0, 0)

    def body(ii, carry):
        for s in range(nsub):
            step(nsub * ii + s, s, 0, False, 0)
        return carry
    lax.fori_loop(0, i, body, 0)

    for s in range(nsub):
        step(nsub * i + s, s, s * bk, True, (s + 1) * bk if s + 1 < nsub else None)

    ot = jnp.concatenate([acc_sc[e, :FOX_DH, :] / acc_sc[e, FOX_DH:FOX_DH + 1, :] for e in range(2)],
                         axis=0)
    o_ref[0] = ot.T.astype(BF16)


def _fox_prompt(q, k, v):
    nq = LP // FOX_BQ
    score_buf = pltpu.VMEM((2, FOX_BLK, FOX_BQ), F32)
    return pl.pallas_call(
        _fox_kernel,
        grid=(BATCH, FOX_HEADS // 2, nq),
        in_specs=[pl.BlockSpec((1, 2, FOX_BQ, LANES), lambda b, p, i: (b, p, i, 0)),
                  pl.BlockSpec((1, 2, LP, LANES), lambda b, p, i: (b, p, 0, 0)),
                  pl.BlockSpec((1, 2, LP, LANES), lambda b, p, i: (b, p, 0, 0))],
        out_specs=pl.BlockSpec((1, FOX_BQ, LANES), lambda b, p, i: (b, i, p)),
        out_shape=jax.ShapeDtypeStruct((BATCH, LP, FOX_W), BF16),
        scratch_shapes=[pltpu.VMEM((2, LP // FOX_BLK, FOX_VROWS, FOX_BLK), BF16),
                        pltpu.VMEM((2, FOX_VROWS, FOX_BQ), F32),
                        pltpu.VMEM((2, 1, FOX_BQ), F32),
                        pltpu.VMEM((FOX_BQ // FOX_BLK, 2, 1, FOX_BQ), F32),
                        score_buf, score_buf, score_buf],
        compiler_params=pltpu.CompilerParams(
            dimension_semantics=("parallel", "parallel", "arbitrary"), vmem_limit_bytes=VMEM_LIMIT),
        name="fox_prompt",
    )(q, k, v)


def _gla_tables():
    c = GLA_CHUNK
    t = np.arange(c)[:, None]
    j = np.arange(c)[None, :]
    mats = [(j <= t).astype(np.float32), (j > t).astype(np.float32)]
    masks = [np.eye(c, dtype=np.float32)]
    blk = c
    while blk >= 2:
        half = blk // 2
        mid = (t // blk) * blk + half
        mats.append((j <= t).astype(np.float32) - (j <= mid).astype(np.float32))
        s = j
        masks.append((((t // blk) == (s // blk)) & ((t % blk) >= half) & ((s % blk) < half))
                     .astype(np.float32))
        blk = half
    return np.concatenate(mats, axis=0), np.stack([np.concatenate([m, m], axis=0) for m in masks])


_GLA_LEVELS = 7


def _gla_kernel(q_ref, k_ref, lg_ref, v_ref, dall_ref, masks_ref, o_ref, sfin_ref, st_sc):
    i = pl.program_id(2)
    c = GLA_CHUNK

    @pl.when(i == 0)
    def _():
        st_sc[...] = jnp.zeros_like(st_sc)

    lane = lax.broadcasted_iota(jnp.int32, (1, LANES), 1)
    hmask = (lane < GLA_DK, lane >= GLA_DK)

    for ci in range(GLA_TILE // c):
        rows = slice(ci * c, (ci + 1) * c)
        q = q_ref[0, rows, :]
        k = k_ref[0, rows, :]
        lg = lg_ref[0, rows, :]
        hi = lg.astype(BF16)
        lo = (lg - hi.astype(F32)).astype(BF16)
        e2 = _dot(dall_ref[...], jnp.concatenate([hi, lo], axis=1))
        ex = e2[:, :LANES] + e2[:, LANES:]
        bc = ex[0:c]
        q_in = q * jnp.exp(bc)
        k_dec = k * jnp.exp(ex[c:2 * c])
        qs = [q]
        ks = [k.astype(BF16)]
        for lv in range(_GLA_LEVELS):
            f = jnp.exp(-jnp.abs(ex[(2 + lv) * c:(3 + lv) * c]))
            qs.append(q * f)
            ks.append((k * f).astype(BF16))
        decay_all = jnp.exp(bc[c - 1:c])
        a2 = jnp.zeros((2 * c, c), F32)
        for lv in range(_GLA_LEVELS + 1):
            ql = jnp.concatenate([jnp.where(hmask[0], qs[lv], 0.0),
                                  jnp.where(hmask[1], qs[lv], 0.0)], axis=0).astype(BF16)
            a2 = a2 + masks_ref[lv] * _dot_nt(ql, ks[lv])
        for h in range(2):
            a = a2[h * c:(h + 1) * c]
            vh = v_ref[0, rows, h * GLA_DV:(h + 1) * GLA_DV]
            st = st_sc[h]
            o = _dot(a.astype(BF16), vh) + _dot_nt(jnp.where(hmask[h], q_in, 0.0).astype(BF16),
                                                   st.astype(BF16))
            o_ref[0, rows, h * GLA_DV:(h + 1) * GLA_DV] = o
            kd = jnp.where(hmask[h], k_dec, 0.0).astype(BF16)
            st_sc[h] = decay_all * st + _dot_tn(vh, kd)

    @pl.when(i == pl.num_programs(2) - 1)
    def _():
        for h in range(2):
            s = st_sc[h].T
            sfin_ref[0, h] = s[h * GLA_DK:(h + 1) * GLA_DK, :]


def _gla_prompt(gq, gk, lg, gv):
    dall_np, masks_np = _gla_tables()
    dall = jnp.asarray(dall_np, BF16)
    masks = jnp.asarray(masks_np, F32)
    t = GLA_TILE
    pair = pl.BlockSpec((1, t, LANES), lambda b, p, i: (b, i, p))
    wide = pl.BlockSpec((1, t, 2 * GLA_DV), lambda b, p, i: (b, i, p))
    return pl.pallas_call(
        _gla_kernel,
        grid=(BATCH, GLA_HEADS // 2, LP // t),
        in_specs=[pair, pair, pair, wide, _const_spec(dall.shape), _const_spec(masks.shape)],
        out_specs=(wide, pl.BlockSpec((1, 2, GLA_DK, GLA_DV), lambda b, p, i: (b, p, 0, 0))),
        out_shape=(jax.ShapeDtypeStruct((BATCH, LP, GLA_V), F32),
                   jax.ShapeDtypeStruct((BATCH, GLA_HEADS, GLA_DK, GLA_DV), F32)),
        scratch_shapes=[pltpu.VMEM((2, GLA_DV, LANES), F32)],
        compiler_params=pltpu.CompilerParams(
            dimension_semantics=("parallel", "parallel", "arbitrary"), vmem_limit_bytes=VMEM_LIMIT),
        name="gla_prompt",
    )(gq, gk, lg, gv, dall, masks)


def _merge_kernel(h_ref, oa_ref, ob_ref, wg_ref, wpa_ref, wpb_ref, wo_ref, gn_ref, g_ref, b_ref,
                  out_ref):
    x = h_ref[...]
    xb = x.astype(BF16)
    r = _dot(xb, wg_ref[...])
    rb = r[:, :GLA_V]
    ga = r[:, GLA_V:GLA_V + D_MODEL]
    gb = r[:, GLA_V + D_MODEL:]
    ob = ob_ref[...]
    parts = []
    for hd in range(GLA_HEADS):
        o = ob[:, hd * GLA_DV:(hd + 1) * GLA_DV]
        ms = jnp.mean(o * o, axis=-1, keepdims=True)
        parts.append(o * lax.rsqrt(ms + NORM_EPS) * gn_ref[...])
    obn = jnp.concatenate(parts, axis=1) * (rb * _sigmoid(rb))
    y_a = _dot(oa_ref[...], wpa_ref[...])
    y_b = _dot(obn.astype(BF16), wpb_ref[...])
    mixed = _sigmoid(ga) * y_a + _sigmoid(gb) * y_b
    y = ALPHA * x + _dot(mixed.astype(BF16), wo_ref[...])
    out_ref[...] = _layer_norm(y, g_ref[...], b_ref[...])


def _merge(h, oa, ob, lw, l, tm):
    m = h.shape[0]
    row = lambda w: pl.BlockSpec((tm, w), lambda i: (i, 0))
    ws = (lw["wmg"], lw["wpa"], lw["wpb"], lw["wo"], lw["gn"], lw["ln1g"], lw["ln1b"])
    return pl.pallas_call(
        _merge_kernel,
        grid=(m // tm,),
        in_specs=[row(D_MODEL), row(FOX_W), row(GLA_V)] + [_layer_spec(w, l) for w in ws],
        out_specs=row(D_MODEL),
        out_shape=jax.ShapeDtypeStruct((m, D_MODEL), F32),
        compiler_params=pltpu.CompilerParams(
            dimension_semantics=("parallel",), vmem_limit_bytes=VMEM_LIMIT),
        name="merge",
    )(h, oa, ob, *ws)


def _ffn_kernel(h_ref, wgate_ref, wup_ref, wdown_ref, g_ref, b_ref, out_ref):
    x = h_ref[...]
    xb = x.astype(BF16)
    gt = _dot(xb, wgate_ref[...])
    up = _dot(xb, wup_ref[...])
    hdn = (gt * _sigmoid(gt) * up).astype(BF16)
    y = ALPHA * x + _dot(hdn, wdown_ref[...])
    out_ref[...] = _layer_norm(y, g_ref[...], b_ref[...])


def _ffn(h, lw, l, tm):
    m = h.shape[0]
    row = pl.BlockSpec((tm, D_MODEL), lambda i: (i, 0))
    ws = (lw["wgate"], lw["wup"], lw["wdown"], lw["ln2g"], lw["ln2b"])
    return pl.pallas_call(
        _ffn_kernel,
        grid=(m // tm,),
        in_specs=[row] + [_layer_spec(w, l) for w in ws],
        out_specs=row,
        out_shape=jax.ShapeDtypeStruct((m, D_MODEL), F32),
        compiler_params=pltpu.CompilerParams(
            dimension_semantics=("parallel",), vmem_limit_bytes=VMEM_LIMIT),
        name="ffn",
    )(h, *ws)


def _sample_inproj_kernel(x_ref, wqkv_ref, wg_ref, ws_ref, w2_ref, bs_ref, ba_ref,
                          r1_ref, r2_ref, lf_ref, lg_ref):
    xb = x_ref[...].astype(BF16)
    r1_ref[...] = _dot(xb, wqkv_ref[...])
    r2_ref[...] = _dot(xb, wg_ref[...])
    rs = _dot(xb, ws_ref[...]) + bs_ref[...]
    lf_ref[...] = _log_sigmoid(rs)
    z = _dot(rs.astype(BF16), w2_ref[...]) + ba_ref[...]
    lg_ref[...] = _log_sigmoid(z) * (1.0 / GLA_TAU)


def _sample_inproj(x, lw, l):
    ws = (lw["wqkv"], lw["wgla"], lw["ws"], lw["w2"], lw["bs"], lw["ba"])
    full = lambda shape: pl.BlockSpec(shape, lambda i: (0,) * len(shape))
    out_shape = (jax.ShapeDtypeStruct((DEC_BATCH, 3 * FOX_W), F32),
                 jax.ShapeDtypeStruct((DEC_BATCH, 2 * GLA_K + GLA_V), F32),
                 jax.ShapeDtypeStruct((DEC_BATCH, LANES), F32),
                 jax.ShapeDtypeStruct((DEC_BATCH, GLA_K), F32))
    return pl.pallas_call(
        _sample_inproj_kernel,
        grid=(1,),
        in_specs=[full(x.shape)] + [_layer_spec(w, l) for w in ws],
        out_specs=tuple(full(s.shape) for s in out_shape),
        out_shape=out_shape,
        compiler_params=pltpu.CompilerParams(vmem_limit_bytes=VMEM_LIMIT),
        name="sample_inproj",
    )(x, *ws)


def _decode_tables():
    j = np.arange(LANES)
    ut = np.concatenate([j[:, None] > j[None, :], np.ones((LANES, LANES), bool)], axis=1)
    nr = PAGES_PER_STEP * FOX_HEADS
    r = np.arange(nr)
    same = (r[:, None] % FOX_HEADS) == (r[None, :] % FOX_HEADS)
    us = np.zeros((nr + 2 * SUBLANES, nr), bool)
    us[:nr] = same & (r[None, :] > r[:, None])
    us[nr:nr + FOX_HEADS] = (r[None, :] % FOX_HEADS) == np.arange(FOX_HEADS)[:, None]
    return ut.astype(np.float32), us.astype(np.float32)


def _fox_decode_kernel(pt_ref, qbd_ref, knb_ref, vrow_ref, lfn_ref, ut_ref, us_ref, *rest):
    npg = PAGES_PER_STEP
    kp = rest[0:npg]
    vp = rest[npg:2 * npg]
    lft_ref = rest[2 * npg]
    o_ref = rest[2 * npg + 1]
    m_sc, l_sc, acc_sc, carry_sc = rest[2 * npg + 2:]
    b = pl.program_id(0)
    j = pl.program_id(1)
    first_page = (pl.num_programs(1) - 1 - j) * npg
    nh = FOX_HEADS
    nr = npg * nh
    qbd = qbd_ref[0]

    def page2d(ref):
        return ref[...].reshape(FOX_W, PAGE_SIZE).astype(BF16)

    @pl.when(j == 0)
    def _():
        m_sc[...] = _dot(qbd, page2d(knb_ref.at[0]))
        lane = lax.broadcasted_iota(jnp.int32, (nh, LANES), 1)
        l_sc[...] = jnp.where(lane == 0, 1.0, 0.0)
        acc_sc[...] = jnp.broadcast_to(vrow_ref[0].astype(BF16).astype(F32), (nh, FOX_W))
        carry_sc[...] = lfn_ref[0]

    lfc = jnp.concatenate([lft_ref[pt_ref[b, first_page + g]] for g in range(npg)],
                          axis=0)
    w = _dot(jnp.concatenate(_split3(lfc), axis=0), ut_ref[...])

    s = [_dot(qbd, page2d(kp[g])) for g in range(npg)]

    wsum = w[0:nr] + w[nr:2 * nr] + w[2 * nr:3 * nr]
    x = _dot(us_ref[...], jnp.concatenate(_split3(wsum[:, LANES:]), axis=1))
    xs = x[:, :LANES] + x[:, LANES:2 * LANES] + x[:, 2 * LANES:]
    carry = carry_sc[...]
    sb = [s[g] + (wsum[g * nh:(g + 1) * nh, :LANES] + xs[g * nh:(g + 1) * nh] + carry)
          for g in range(npg)]
    carry_sc[...] = carry + xs[nr:nr + nh]

    mx = sb[0]
    for g in range(1, npg):
        mx = jnp.maximum(mx, sb[g])
    m_old = m_sc[...]
    m_new = jnp.maximum(m_old, jnp.max(mx, axis=1, keepdims=True))
    alpha = jnp.exp(m_old - m_new)
    p = [jnp.exp(sb[g] - m_new) for g in range(npg)]
    psum = p[0]
    for g in range(1, npg):
        psum = psum + p[g]
    l_sc[...] = alpha * l_sc[...] + psum
    m_sc[...] = m_new
    pv = _dot_nt(p[0].astype(BF16), page2d(vp[0]))
    for g in range(1, npg):
        pv = pv + _dot_nt(p[g].astype(BF16), page2d(vp[g]))
    acc_sc[...] = jnp.concatenate([alpha] * (FOX_W // LANES), axis=1) * acc_sc[...] + pv

    @pl.when(j == pl.num_programs(1) - 1)
    def _():
        ltot = jnp.sum(l_sc[...], axis=1, keepdims=True)
        own = (lax.broadcasted_iota(jnp.int32, (nh, FOX_W), 1) // FOX_DH
               == lax.broadcasted_iota(jnp.int32, (nh, FOX_W), 0))
        o_ref[0] = jnp.sum(jnp.where(own, acc_sc[...] / ltot, 0.0), axis=0, keepdims=True)


def _fox_decode(layer, page_table, qbd, knb, vrow, lfn, cache_kt, cache_vt, cache_lft):
    npg = PAGES_PER_STEP
    nsteps = N_PAGES // npg
    ut_np, us_np = _decode_tables()
    ut = jnp.asarray(ut_np, BF16)
    us = jnp.asarray(us_np, BF16)

    def page_map(g, tail):
        def f(b, j, pt):
            return (layer, pt[b, (nsteps - 1 - j) * npg + g]) + tail
        return f

    kv_specs = [pl.BlockSpec((None, None, FOX_HEADS, FOX_DH, PAGE_SIZE), page_map(g, (0, 0, 0)))
                for g in range(npg)]
    lf_spec = pl.BlockSpec((None,) + cache_lft.shape[1:], lambda b, j, pt: (layer, 0, 0, 0),
                           pipeline_mode=pl.Buffered(1))
    per_b = lambda shape: pl.BlockSpec((1,) + shape, lambda b, j, pt: (b,) + (0,) * len(shape))
    const = lambda shape: pl.BlockSpec(shape, lambda b, j, pt: (0,) * len(shape))
    hdl = (FOX_HEADS, FOX_DH, LANES)
    grid_spec = pltpu.PrefetchScalarGridSpec(
        num_scalar_prefetch=1,
        grid=(DEC_BATCH, nsteps),
        in_specs=[per_b((FOX_HEADS, FOX_W)), per_b(hdl), per_b((1, FOX_W)),
                  per_b((FOX_HEADS, LANES)), const(ut.shape), const(us.shape)]
        + kv_specs + kv_specs + [lf_spec],
        out_specs=per_b((1, FOX_W)),
        scratch_shapes=[pltpu.VMEM((FOX_HEADS, LANES), F32), pltpu.VMEM((FOX_HEADS, LANES), F32),
                        pltpu.VMEM((FOX_HEADS, FOX_W), F32), pltpu.VMEM((FOX_HEADS, LANES), F32)],
    )
    return pl.pallas_call(
        _fox_decode_kernel,
        grid_spec=grid_spec,
        out_shape=jax.ShapeDtypeStruct((DEC_BATCH, 1, FOX_W), F32),
        compiler_params=pltpu.CompilerParams(
            dimension_semantics=("parallel", "arbitrary"), vmem_limit_bytes=VMEM_LIMIT),
        name="fox_decode",
    )(page_table, qbd, knb, vrow, lfn, ut, us,
      *([cache_kt] * npg), *([cache_vt] * npg), cache_lft)


def _gla_decode_kernel(q_ref, k_ref, g_ref, v_ref, s_ref, o_ref, sn_ref):
    eye = (lax.broadcasted_iota(jnp.int32, (GLA_DK, GLA_DK), 0)
           == lax.broadcasted_iota(jnp.int32, (GLA_DK, GLA_DK), 1))

    def col(r):
        return jnp.sum(jnp.where(eye, jnp.broadcast_to(r, (GLA_DK, GLA_DK)), 0.0),
                       axis=1, keepdims=True)

    for r in range(GLA_DEC_ROWS):
        for h in range(GLA_HEADS):
            qc = col(q_ref[r, h:h + 1, :])
            kc = col(k_ref[r, h:h + 1, :])
            ac = col(jnp.exp(g_ref[r, h:h + 1, :]))
            sn = ac * s_ref[r, h] + kc * v_ref[r, h:h + 1, :]
            sn_ref[r, h] = sn
            o_ref[r, h:h + 1, :] = jnp.sum(qc * sn, axis=0, keepdims=True)


def _gla_decode(gq, gk, lg, gv, state):
    nb = GLA_DEC_ROWS
    hk = pl.BlockSpec((nb, GLA_HEADS, GLA_DK), lambda b: (b, 0, 0))
    hv = pl.BlockSpec((nb, GLA_HEADS, GLA_DV), lambda b: (b, 0, 0))
    st = pl.BlockSpec((nb, GLA_HEADS, GLA_DK, GLA_DV), lambda b: (b, 0, 0, 0))
    return pl.pallas_call(
        _gla_decode_kernel,
        grid=(DEC_BATCH // nb,),
        in_specs=[hk, hk, hk, hv, st],
        out_specs=(hv, st),
        out_shape=(jax.ShapeDtypeStruct((DEC_BATCH, GLA_HEADS, GLA_DV), F32),
                   jax.ShapeDtypeStruct((DEC_BATCH, GLA_HEADS, GLA_DK, GLA_DV), F32)),
        compiler_params=pltpu.CompilerParams(
            dimension_semantics=("parallel",), vmem_limit_bytes=VMEM_LIMIT),
        name="gla_decode",
    )(gq, gk, lg, gv, state)


def _stacked_weights(w_in, b_f, w_a2, b_a, gla_norm_g, w_pa, w_pb, w_o, ln1_g, ln1_b,
                     w_gate, w_up, w_down, ln2_g, ln2_b):
    o = _OFF
    sc_f = FOX_DH ** -0.5
    sc_g = GLA_DK ** -0.5
    seg = lambda a: w_in[:, :, o[a]:o[a + 1]]
    wqkv = jnp.concatenate([seg(0) * sc_f, seg(1), seg(2)], axis=2).astype(BF16)
    wgla = jnp.concatenate([seg(4) * sc_g, seg(5), seg(6)], axis=2).astype(BF16)
    ws = jnp.zeros((DEPTH, D_MODEL, LANES), F32)
    ws = ws.at[:, :, :FOX_HEADS].set(seg(3)).at[:, :, FOX_HEADS:FOX_HEADS + GLA_RANK].set(seg(8))
    w2 = jnp.zeros((DEPTH, LANES, GLA_K), F32).at[:, FOX_HEADS:FOX_HEADS + GLA_RANK].set(w_a2)
    bs = jnp.zeros((DEPTH, 1, LANES), F32).at[:, 0, :FOX_HEADS].set(b_f)
    wmg = jnp.concatenate([seg(7), seg(9), seg(10)], axis=2).astype(BF16)
    vec = lambda t: t[:, None, :]
    return dict(
        wqkv=wqkv, wgla=wgla, ws=ws.astype(BF16), w2=w2.astype(BF16), bs=bs, ba=vec(b_a),
        wmg=wmg, wpa=w_pa.astype(BF16), wpb=w_pb.astype(BF16), wo=w_o.astype(BF16),
        gn=vec(gla_norm_g), ln1g=vec(ln1_g), ln1b=vec(ln1_b),
        wgate=w_gate.astype(BF16), wup=w_up.astype(BF16), wdown=w_down.astype(BF16),
        ln2g=vec(ln2_g), ln2b=vec(ln2_b))


def kernel(x_prompt, x_sample, cache_k, cache_v, cache_lf, state_gla, page_table, meta, w_in, b_f,
           w_a2, b_a, gla_norm_g, w_pa, w_pb, w_o, ln1_g, ln1_b, w_gate, w_up, w_down, ln2_g, ln2_b):
    assert x_prompt.shape == (BATCH, SEQ, D_MODEL) and x_sample.shape == (DEC_BATCH, 1, D_MODEL)
    hp = jnp.concatenate([jnp.broadcast_to(meta.astype(F32)[None], (BATCH, N_META, D_MODEL)), x_prompt,
                          jnp.zeros((BATCH, PAD_ROWS, D_MODEL), F32)], axis=1)
    hs = x_sample.reshape(DEC_BATCH, D_MODEL)
    cache_kt = jnp.transpose(cache_k, (0, 1, 3, 4, 2))
    cache_vt = jnp.transpose(cache_v, (0, 1, 3, 4, 2))
    cache_lft = jnp.transpose(cache_lf, (0, 1, 3, 2))

    lw = _stacked_weights(w_in, b_f, w_a2, b_a, gla_norm_g, w_pa, w_pb, w_o, ln1_g, ln1_b,
                          w_gate, w_up, w_down, ln2_g, ln2_b)
    kp, vp, lfp, gp, ksr, vsr, lfs, gs = [], [], [], [], [], [], [], []
    for l in range(DEPTH):
        qa, ka, va, kf, vf, lf, gq, gk, gv, lg = _inproj(hp, lw, l)
        oa = _fox_prompt(qa, ka, va)
        ob, sfin = _gla_prompt(gq, gk, lg, gv)
        h2 = hp.reshape(BATCH * LP, D_MODEL)
        h2 = _merge(h2, oa.reshape(BATCH * LP, FOX_W), ob.reshape(BATCH * LP, GLA_V), lw, l, TM_TOK)
        h2 = _ffn(h2, lw, l, TM_TOK)
        hp = h2.reshape(BATCH, LP, D_MODEL)
        kp.append(kf)
        vp.append(vf)
        lfp.append(lf[:, :L_REAL])
        gp.append(sfin)
        r1, r2, lfs_full, lgs = _sample_inproj(hs, lw, l)
        q_s = r1[:, :FOX_W].reshape(DEC_BATCH, FOX_HEADS, FOX_DH)
        k_s = r1[:, FOX_W:2 * FOX_W].reshape(DEC_BATCH, FOX_HEADS, FOX_DH)
        v_s = r1[:, 2 * FOX_W:].reshape(DEC_BATCH, FOX_HEADS, FOX_DH)
        lf_s = lfs_full[:, :FOX_HEADS]
        lanes = lambda t: jnp.broadcast_to(t[..., None], t.shape + (LANES,))
        eye = jnp.eye(FOX_HEADS, dtype=F32)
        qbd = (q_s[:, :, None, :] * eye[None, :, :, None]).reshape(DEC_BATCH, FOX_HEADS, FOX_W)
        oa_s = _fox_decode(l, page_table, qbd.astype(BF16), lanes(k_s), r1[:, None, 2 * FOX_W:],
                           lanes(lf_s), cache_kt, cache_vt, cache_lft)
        ob_s, s_new = _gla_decode(r2[:, :GLA_K].reshape(DEC_BATCH, GLA_HEADS, GLA_DK),
                                  r2[:, GLA_K:2 * GLA_K].reshape(DEC_BATCH, GLA_HEADS, GLA_DK),
                                  lgs.reshape(DEC_BATCH, GLA_HEADS, GLA_DK),
                                  r2[:, 2 * GLA_K:].reshape(DEC_BATCH, GLA_HEADS, GLA_DV),
                                  state_gla[l])
        hs = _merge(hs, oa_s.reshape(DEC_BATCH, FOX_W).astype(BF16),
                    ob_s.reshape(DEC_BATCH, GLA_V), lw, l, DEC_BATCH)
        hs = _ffn(hs, lw, l, DEC_BATCH)
        ksr.append(k_s[:, None])
        vsr.append(v_s[:, None])
        lfs.append(lf_s[:, None])
        gs.append(s_new)

    y_prompt = hp[:, N_META:L_REAL]
    y_sample = hs[:, None, :]
    rows = lambda ts: jnp.transpose(
        jnp.stack(ts).reshape(DEPTH, BATCH, FOX_HEADS, FOX_DH, L_REAL), (0, 1, 4, 2, 3))
    return (y_prompt, y_sample, rows(kp), rows(vp), jnp.stack(lfp), jnp.stack(gp),
            jnp.stack(ksr), jnp.stack(vsr), jnp.stack(lfs), jnp.stack(gs))
```

```python
import functools

import numpy as np
import jax
import jax.numpy as jnp
from jax import lax
from jax.experimental import pallas as pl
from jax.experimental.pallas import tpu as pltpu

D_MODEL = 1024
BATCH = 2
SEQ = 8192
DEPTH = 2
DEC_BATCH = 32
PAST_LEN = 8192
PAGE_SIZE = 128
N_META = 16
FOX_HEADS = 8
FOX_DH = 64
FOX_W = FOX_HEADS * FOX_DH
GLA_HEADS = 4
GLA_DK = 64
GLA_DV = 128
GLA_K = GLA_HEADS * GLA_DK
GLA_V = GLA_HEADS * GLA_DV
GLA_RANK = 16
GLA_TAU = 16.0
D_FF = 2816
LN_EPS = 1e-5
NORM_EPS = 1e-6
NEG_INF = -1e30
ALPHA = (2.0 * DEPTH) ** 0.25
_SPLITS = (FOX_W, FOX_W, FOX_W, FOX_HEADS, GLA_K, GLA_K, GLA_V, GLA_V, GLA_RANK, D_MODEL, D_MODEL)
_OFF = np.concatenate([[0], np.cumsum(_SPLITS)]).tolist()

LANES = 128
SUBLANES = 8
VMEM_LIMIT = 56 * 1024 * 1024

L_REAL = SEQ + N_META
FOX_BLK = 256
FOX_BQ = 768
LOG2E = float(np.log2(np.e))
LP = -(-L_REAL // FOX_BLK) * FOX_BLK
PAD_ROWS = LP - L_REAL
GLA_CHUNK = 128
GLA_TILE = 768
TM_IN = 384
CUM_BLK = 128
TM_TOK = 512
N_PAGES = PAST_LEN // PAGE_SIZE
PAGES_PER_STEP = 16
GLA_DEC_ROWS = 8

F32 = jnp.float32
BF16 = jnp.bfloat16


def _dot(a, b):
    return jnp.dot(a, b, preferred_element_type=F32)


def _dot_nt(a, b):
    return lax.dot_general(a, b, (((1,), (1,)), ((), ())), preferred_element_type=F32)


def _dot_tn(a, b):
    return lax.dot_general(a, b, (((0,), (0,)), ((), ())), preferred_element_type=F32)


def _log_sigmoid(x):
    return jnp.minimum(x, 0.0) - jnp.log(1.0 + jnp.exp(-jnp.abs(x)))


def _sigmoid(x):
    return 1.0 / (1.0 + jnp.exp(-x))


def _layer_norm(y, g, b):
    mu = jnp.mean(y, axis=-1, keepdims=True)
    d = y - mu
    var = jnp.mean(d * d, axis=-1, keepdims=True)
    return d * lax.rsqrt(var + LN_EPS) * g + b


def _split3(x):
    hi = x.astype(BF16)
    r = x - hi.astype(F32)
    mid = r.astype(BF16)
    lo = (r - mid.astype(F32)).astype(BF16)
    return hi, mid, lo


def _fox_place_table():
    nh = FOX_HEADS
    t = np.zeros((LANES, 2 * nh * LANES), np.float32)
    for h in range(nh):
        xo = FOX_DH if h % 2 == 0 else 0
        qc = h * LANES + xo
        kc = (nh + h) * LANES + xo
        for part in range(3):
            t[part * nh + h, qc + part] = 1.0
            t[3 * nh, qc + 3 + part] = 1.0
            t[3 * nh, kc + part] = 1.0
            t[part * nh + h, kc + 3 + part] = -1.0
    return t


def _inproj_kernel(x_ref, wqkv_ref, wg_ref, ws_ref, w2_ref, bs_ref, ba_ref, tri_ref, place_ref,
                   q_ref, k_ref, v_ref, kf_ref, vf_ref, lf_ref, gq_ref, gk_ref, gv_ref, lg_ref,
                   carry_ref, *, tm):
    i = pl.program_id(1)

    @pl.when(i == 0)
    def _():
        carry_ref[...] = jnp.zeros_like(carry_ref)

    row = i * tm + lax.broadcasted_iota(jnp.int32, (tm, 1), 0)
    real = row < L_REAL
    xb = jnp.where(real, x_ref[0], 0.0).astype(BF16)

    rs = _dot(xb, ws_ref[...]) + bs_ref[...]
    lf_full = jnp.where(real, _log_sigmoid(rs), 0.0)
    lf_ref[0] = lf_full[:, :FOX_HEADS]
    z = _dot(rs.astype(BF16), w2_ref[...]) + ba_ref[...]
    lg_ref[0] = jnp.where(real, _log_sigmoid(z) * (1.0 / GLA_TAU), 0.0)

    carry = carry_ref[...]
    tri = tri_ref[...]
    cs = []
    for sb in range(tm // CUM_BLK):
        hi, mid, lo = _split3(lf_full[sb * CUM_BLK:(sb + 1) * CUM_BLK])
        c = _dot(tri, hi) + _dot(tri, mid) + _dot(tri, lo) + carry
        carry = c[CUM_BLK - 1:CUM_BLK]
        cs.append(c)
    carry_ref[...] = carry
    c = jnp.concatenate(cs, axis=0) * LOG2E

    r = _dot(xb, wqkv_ref[...])
    kf_ref[0] = r[:, FOX_W:2 * FOX_W].T
    vf_ref[0] = r[:, 2 * FOX_W:3 * FOX_W].T

    lane = lax.broadcasted_iota(jnp.int32, (1, LANES), 1)
    hi = c.astype(BF16).astype(F32)
    r1 = c - hi
    mid = r1.astype(BF16).astype(F32)
    lo = r1 - mid
    nh = FOX_HEADS
    parts = jnp.where(lane < nh, hi, jnp.where(lane < 2 * nh, pltpu.roll(mid, nh, axis=1), jnp.where(
        lane < 3 * nh, pltpu.roll(lo, 2 * nh, axis=1), jnp.where(lane == 3 * nh, 1.0, 0.0))))
    ext = _dot(parts.astype(BF16), place_ref[...])
    for h in range(FOX_HEADS):
        p, e = divmod(h, 2)
        dmask = (lane < FOX_DH) if e == 0 else (lane >= FOX_DH)
        xo = FOX_DH if e == 0 else 0
        ev = jnp.where(lane == xo, 1.0, 0.0)
        rq = r[:, p * LANES:(p + 1) * LANES]
        rk = r[:, FOX_W + p * LANES:FOX_W + (p + 1) * LANES]
        rv = r[:, 2 * FOX_W + p * LANES:2 * FOX_W + (p + 1) * LANES]
        q_ref[0, h] = jnp.where(dmask, rq * LOG2E, ext[:, h * LANES:(h + 1) * LANES]).astype(BF16)
        k_ref[0, h] = jnp.where(dmask, rk, ext[:, (nh + h) * LANES:(nh + h + 1) * LANES]).astype(BF16)
        vt = jnp.where(dmask, rv, ev)
        for cb in range(tm // LANES):
            v_ref[0, h, cb] = vt[cb * LANES:(cb + 1) * LANES].T.astype(BF16)

    rg = _dot(xb, wg_ref[...])
    gq_ref[0] = rg[:, :GLA_K]
    gk_ref[0] = rg[:, GLA_K:2 * GLA_K]
    gv_ref[0] = rg[:, 2 * GLA_K:].astype(BF16)


def _const_spec(shape):
    nd = len(shape)
    return pl.BlockSpec(shape, lambda *_: (0,) * nd, pipeline_mode=pl.Buffered(1))


def _layer_spec(w, l):
    nd = w.ndim - 1
    return pl.BlockSpec((None,) + w.shape[1:], lambda *_: (l,) + (0,) * nd,
                        pipeline_mode=pl.Buffered(1))


def _inproj(hp, lw, l):
    tm = TM_IN
    nt = LP // tm
    tri = jnp.asarray(np.tril(np.ones((CUM_BLK, CUM_BLK), np.float32)), BF16)
    place = jnp.asarray(_fox_place_table(), BF16)
    row3 = lambda w: pl.BlockSpec((1, tm, w), lambda b, i: (b, i, 0))
    head4 = pl.BlockSpec((1, FOX_HEADS, tm, LANES), lambda b, i: (b, 0, i, 0))
    col3 = pl.BlockSpec((1, FOX_W, tm), lambda b, i: (b, 0, i))
    vchunks = pl.BlockSpec((1, FOX_HEADS, tm // LANES, LANES, LANES), lambda b, i: (b, 0, i, 0, 0))
    out_shape = (
        jax.ShapeDtypeStruct((BATCH, FOX_HEADS, LP, LANES), BF16),
        jax.ShapeDtypeStruct((BATCH, FOX_HEADS, LP, LANES), BF16),
        jax.ShapeDtypeStruct((BATCH, FOX_HEADS, LP // LANES, LANES, LANES), BF16),
        jax.ShapeDtypeStruct((BATCH, FOX_W, L_REAL), F32),
        jax.ShapeDtypeStruct((BATCH, FOX_W, L_REAL), F32),
        jax.ShapeDtypeStruct((BATCH, LP, FOX_HEADS), F32),
        jax.ShapeDtypeStruct((BATCH, LP, GLA_K), F32),
        jax.ShapeDtypeStruct((BATCH, LP, GLA_K), F32),
        jax.ShapeDtypeStruct((BATCH, LP, GLA_V), BF16),
        jax.ShapeDtypeStruct((BATCH, LP, GLA_K), F32),
    )
    return pl.pallas_call(
        functools.partial(_inproj_kernel, tm=tm),
        grid=(BATCH, nt),
        in_specs=[row3(D_MODEL)]
        + [_layer_spec(lw[n], l) for n in ("wqkv", "wgla", "ws", "w2", "bs", "ba")]
        + [_const_spec(tri.shape), _const_spec(place.shape)],
        out_specs=(head4, head4, vchunks, col3, col3, row3(FOX_HEADS),
                   row3(GLA_K), row3(GLA_K), row3(GLA_V), row3(GLA_K)),
        out_shape=out_shape,
        scratch_shapes=[pltpu.VMEM((1, LANES), F32)],
        compiler_params=pltpu.CompilerParams(
            dimension_semantics=("parallel", "arbitrary"), vmem_limit_bytes=VMEM_LIMIT),
        name="prompt_inproj",
    )(hp, lw["wqkv"], lw["wgla"], lw["ws"], lw["w2"], lw["bs"], lw["ba"], tri, place)


def _fox_kernel(q_ref, k_ref, v_ref, o_ref, acc_sc, m_sc, mb_sc, s0_sc, s1_sc, s2_sc):
    i = pl.program_id(2)
    bq, bk = FOX_BQ, FOX_BLK
    nsub = bq // bk
    per_blk = bk // LANES

    for e in range(2):
        m_sc[e] = jnp.full((1, bq), NEG_INF, F32)
        acc_sc[e] = jnp.zeros((LANES, bq), F32)

    def values_t(e, j):
        return jnp.concatenate([v_ref[0, e, per_blk * j + s] for s in range(per_blk)], axis=1)

    sbuf = (s0_sc, s1_sc, s2_sc)
    causal = (lax.broadcasted_iota(jnp.int32, (bk, 1), 0)
              <= lax.broadcasted_iota(jnp.int32, (1, bk), 1))

    def scores(j, slot, c0):
        start = j * bk if isinstance(j, int) else pl.multiple_of(j * bk, bk)
        for e in range(2):
            st = _dot_nt(k_ref[0, e, pl.ds(start, bk), :], q_ref[0, e, c0:, :])
            sbuf[slot][e, :, c0:] = st
            mb_sc[slot, e, :, c0:] = jnp.max(st, axis=0, keepdims=True)

    def step(j, cur, c0, masked, next_c0):
        if next_c0 is not None:
            scores(j + 1, (cur + 1) % nsub, next_c0)
        for e in range(2):
            for sb in range((bq - c0) // bk):
                cols = slice(c0 + sb * bk, c0 + (sb + 1) * bk)
                st = sbuf[cur][e, :, cols]
                if masked and sb == 0:
                    st = jnp.where(causal, st, NEG_INF)
                    mblk = jnp.max(st, axis=0, keepdims=True)
                else:
                    mblk = mb_sc[cur, e, :, cols]
                m_old = m_sc[e, :, cols]
                m_new = jnp.maximum(m_old, mblk)
                alpha = jnp.exp2(m_old - m_new)
                pt = jnp.exp2(st - m_new).astype(BF16)
                acc_sc[e, :, cols] = alpha * acc_sc[e, :, cols] + _dot(values_t(e, j), pt)
                m_sc[e, :, cols] = m_new

    scores(0, 0, 0)

    def body(ii, carry):
        for s in range(nsub):
            step(nsub * ii + s, s, 0, False, 0)
        return carry
    lax.fori_loop(0, i, body, 0)

    for s in range(nsub):
        step(nsub * i + s, s, s * bk, True, (s + 1) * bk if s + 1 < nsub else None)

    a0 = acc_sc[0]
    a1 = acc_sc[1]
    row = lax.broadcasted_iota(jnp.int32, (LANES, 1), 0)
    ot = jnp.where(row < FOX_DH, a0 / a0[FOX_DH:FOX_DH + 1, :], a1 / a1[0:1, :])
    o_ref[0] = ot.T.astype(BF16)


def _fox_prompt(q, k, v):
    nq = LP // FOX_BQ
    score_buf = pltpu.VMEM((2, FOX_BLK, FOX_BQ), F32)
    return pl.pallas_call(
        _fox_kernel,
        grid=(BATCH, FOX_HEADS // 2, nq),
        in_specs=[pl.BlockSpec((1, 2, FOX_BQ, LANES), lambda b, p, i: (b, p, i, 0)),
                  pl.BlockSpec((1, 2, LP, LANES), lambda b, p, i: (b, p, 0, 0)),
                  pl.BlockSpec((1, 2, LP // LANES, LANES, LANES), lambda b, p, i: (b, p, 0, 0, 0))],
        out_specs=pl.BlockSpec((1, FOX_BQ, LANES), lambda b, p, i: (b, i, p)),
        out_shape=jax.ShapeDtypeStruct((BATCH, LP, FOX_W), BF16),
        scratch_shapes=[pltpu.VMEM((2, LANES, FOX_BQ), F32),
                        pltpu.VMEM((2, 1, FOX_BQ), F32),
                        pltpu.VMEM((FOX_BQ // FOX_BLK, 2, 1, FOX_BQ), F32),
                        score_buf, score_buf, score_buf],
        compiler_params=pltpu.CompilerParams(
            dimension_semantics=("parallel", "parallel", "arbitrary"), vmem_limit_bytes=VMEM_LIMIT),
        name="fox_prompt",
    )(q, k, v)


def _gla_tables():
    c = GLA_CHUNK
    t = np.arange(c)[:, None]
    j = np.arange(c)[None, :]
    mats = [(j <= t).astype(np.float32), (j > t).astype(np.float32)]
    masks = [np.eye(c, dtype=np.float32)]
    blk = c
    while blk >= 2:
        half = blk // 2
        mid = (t // blk) * blk + half
        mats.append((j <= t).astype(np.float32) - (j <= mid).astype(np.float32))
        s = j
        masks.append((((t // blk) == (s // blk)) & ((t % blk) >= half) & ((s % blk) < half))
                     .astype(np.float32))
        blk = half
    return np.concatenate(mats, axis=0), np.stack([np.concatenate([m, m], axis=0) for m in masks])


_GLA_LEVELS = 7


def _gla_kernel(q_ref, k_ref, lg_ref, v_ref, dall_ref, masks_ref, o_ref, sfin_ref, st_sc):
    i = pl.program_id(2)
    c = GLA_CHUNK

    @pl.when(i == 0)
    def _():
        st_sc[...] = jnp.zeros_like(st_sc)

    lane = lax.broadcasted_iota(jnp.int32, (1, LANES), 1)
    hmask = (lane < GLA_DK, lane >= GLA_DK)

    for ci in range(GLA_TILE // c):
        rows = slice(ci * c, (ci + 1) * c)
        q = q_ref[0, rows, :]
        k = k_ref[0, rows, :]
        lg = lg_ref[0, rows, :]
        hi = lg.astype(BF16)
        lo = (lg - hi.astype(F32)).astype(BF16)
        e2 = _dot(dall_ref[...], jnp.concatenate([hi, lo], axis=1))
        ex = e2[:, :LANES] + e2[:, LANES:]
        bc = ex[0:c]
        q_in = q * jnp.exp(bc)
        k_dec = k * jnp.exp(ex[c:2 * c])
        qs = [q]
        ks = [k.astype(BF16)]
        for lv in range(_GLA_LEVELS):
            f = jnp.exp(-jnp.abs(ex[(2 + lv) * c:(3 + lv) * c]))
            qs.append(q * f)
            ks.append((k * f).astype(BF16))
        decay_all = jnp.exp(bc[c - 1:c])
        a2 = jnp.zeros((2 * c, c), F32)
        for lv in range(_GLA_LEVELS + 1):
            ql = jnp.concatenate([jnp.where(hmask[0], qs[lv], 0.0),
                                  jnp.where(hmask[1], qs[lv], 0.0)], axis=0).astype(BF16)
            a2 = a2 + masks_ref[lv] * _dot_nt(ql, ks[lv])
        for h in range(2):
            a = a2[h * c:(h + 1) * c]
            vh = v_ref[0, rows, h * GLA_DV:(h + 1) * GLA_DV]
            st = st_sc[h]
            o = _dot(a.astype(BF16), vh) + _dot_nt(jnp.where(hmask[h], q_in, 0.0).astype(BF16),
                                                   st.astype(BF16))
            o_ref[0, rows, h * GLA_DV:(h + 1) * GLA_DV] = o
            kd = jnp.where(hmask[h], k_dec, 0.0).astype(BF16)
            st_sc[h] = decay_all * st + _dot_tn(vh, kd)

    @pl.when(i == pl.num_programs(2) - 1)
    def _():
        for h in range(2):
            s = st_sc[h].T
            sfin_ref[0, h] = s[h * GLA_DK:(h + 1) * GLA_DK, :]


def _gla_prompt(gq, gk, lg, gv):
    dall_np, masks_np = _gla_tables()
    dall = jnp.asarray(dall_np, BF16)
    masks = jnp.asarray(masks_np, F32)
    t = GLA_TILE
    pair = pl.BlockSpec((1, t, LANES), lambda b, p, i: (b, i, p))
    wide = pl.BlockSpec((1, t, 2 * GLA_DV), lambda b, p, i: (b, i, p))
    return pl.pallas_call(
        _gla_kernel,
        grid=(BATCH, GLA_HEADS // 2, LP // t),
        in_specs=[pair, pair, pair, wide, _const_spec(dall.shape), _const_spec(masks.shape)],
        out_specs=(wide, pl.BlockSpec((1, 2, GLA_DK, GLA_DV), lambda b, p, i: (b, p, 0, 0))),
        out_shape=(jax.ShapeDtypeStruct((BATCH, LP, GLA_V), F32),
                   jax.ShapeDtypeStruct((BATCH, GLA_HEADS, GLA_DK, GLA_DV), F32)),
        scratch_shapes=[pltpu.VMEM((2, GLA_DV, LANES), F32)],
        compiler_params=pltpu.CompilerParams(
            dimension_semantics=("parallel", "parallel", "arbitrary"), vmem_limit_bytes=VMEM_LIMIT),
        name="gla_prompt",
    )(gq, gk, lg, gv, dall, masks)


def _merge_kernel(h_ref, oa_ref, ob_ref, wg_ref, wpa_ref, wpb_ref, wo_ref, gn_ref, g_ref, b_ref,
                  out_ref):
    x = h_ref[...]
    xb = x.astype(BF16)
    r = _dot(xb, wg_ref[...])
    rb = r[:, :GLA_V]
    ga = r[:, GLA_V:GLA_V + D_MODEL]
    gb = r[:, GLA_V + D_MODEL:]
    ob = ob_ref[...]
    parts = []
    for hd in range(GLA_HEADS):
        o = ob[:, hd * GLA_DV:(hd + 1) * GLA_DV]
        ms = jnp.mean(o * o, axis=-1, keepdims=True)
        parts.append(o * lax.rsqrt(ms + NORM_EPS) * gn_ref[...])
    obn = jnp.concatenate(parts, axis=1) * (rb * _sigmoid(rb))
    y_a = _dot(oa_ref[...], wpa_ref[...])
    y_b = _dot(obn.astype(BF16), wpb_ref[...])
    mixed = _sigmoid(ga) * y_a + _sigmoid(gb) * y_b
    y = ALPHA * x + _dot(mixed.astype(BF16), wo_ref[...])
    out_ref[...] = _layer_norm(y, g_ref[...], b_ref[...])


def _merge(h, oa, ob, lw, l, tm):
    m = h.shape[0]
    row = lambda w: pl.BlockSpec((tm, w), lambda i: (i, 0))
    ws = (lw["wmg"], lw["wpa"], lw["wpb"], lw["wo"], lw["gn"], lw["ln1g"], lw["ln1b"])
    return pl.pallas_call(
        _merge_kernel,
        grid=(m // tm,),
        in_specs=[row(D_MODEL), row(FOX_W), row(GLA_V)] + [_layer_spec(w, l) for w in ws],
        out_specs=row(D_MODEL),
        out_shape=jax.ShapeDtypeStruct((m, D_MODEL), F32),
        compiler_params=pltpu.CompilerParams(
            dimension_semantics=("parallel",), vmem_limit_bytes=VMEM_LIMIT),
        name="merge",
    )(h, oa, ob, *ws)


def _ffn_kernel(h_ref, wgate_ref, wup_ref, wdown_ref, g_ref, b_ref, out_ref):
    x = h_ref[...]
    xb = x.astype(BF16)
    gt = _dot(xb, wgate_ref[...])
    up = _dot(xb, wup_ref[...])
    hdn = (gt * _sigmoid(gt) * up).astype(BF16)
    y = ALPHA * x + _dot(hdn, wdown_ref[...])
    out_ref[...] = _layer_norm(y, g_ref[...], b_ref[...])


def _ffn(h, lw, l, tm):
    m = h.shape[0]
    row = pl.BlockSpec((tm, D_MODEL), lambda i: (i, 0))
    ws = (lw["wgate"], lw["wup"], lw["wdown"], lw["ln2g"], lw["ln2b"])
    return pl.pallas_call(
        _ffn_kernel,
        grid=(m // tm,),
        in_specs=[row] + [_layer_spec(w, l) for w in ws],
        out_specs=row,
        out_shape=jax.ShapeDtypeStruct((m, D_MODEL), F32),
        compiler_params=pltpu.CompilerParams(
            dimension_semantics=("parallel",), vmem_limit_bytes=VMEM_LIMIT),
        name="ffn",
    )(h, *ws)


def _sample_inproj_kernel(x_ref, wqkv_ref, wg_ref, ws_ref, w2_ref, bs_ref, ba_ref,
                          r1_ref, r2_ref, lf_ref, lg_ref):
    xb = x_ref[...].astype(BF16)
    r1_ref[...] = _dot(xb, wqkv_ref[...])
    r2_ref[...] = _dot(xb, wg_ref[...])
    rs = _dot(xb, ws_ref[...]) + bs_ref[...]
    lf_ref[...] = _log_sigmoid(rs)
    z = _dot(rs.astype(BF16), w2_ref[...]) + ba_ref[...]
    lg_ref[...] = _log_sigmoid(z) * (1.0 / GLA_TAU)


def _sample_inproj(x, lw, l):
    ws = (lw["wqkv"], lw["wgla"], lw["ws"], lw["w2"], lw["bs"], lw["ba"])
    full = lambda shape: pl.BlockSpec(shape, lambda i: (0,) * len(shape))
    out_shape = (jax.ShapeDtypeStruct((DEC_BATCH, 3 * FOX_W), F32),
                 jax.ShapeDtypeStruct((DEC_BATCH, 2 * GLA_K + GLA_V), F32),
                 jax.ShapeDtypeStruct((DEC_BATCH, LANES), F32),
                 jax.ShapeDtypeStruct((DEC_BATCH, GLA_K), F32))
    return pl.pallas_call(
        _sample_inproj_kernel,
        grid=(1,),
        in_specs=[full(x.shape)] + [_layer_spec(w, l) for w in ws],
        out_specs=tuple(full(s.shape) for s in out_shape),
        out_shape=out_shape,
        compiler_params=pltpu.CompilerParams(vmem_limit_bytes=VMEM_LIMIT),
        name="sample_inproj",
    )(x, *ws)


def _decode_tables():
    j = np.arange(LANES)
    ut = np.concatenate([j[:, None] > j[None, :], np.ones((LANES, LANES), bool)], axis=1)
    nr = PAGES_PER_STEP * FOX_HEADS
    r = np.arange(nr)
    same = (r[:, None] % FOX_HEADS) == (r[None, :] % FOX_HEADS)
    us = np.zeros((nr + 2 * SUBLANES, nr), bool)
    us[:nr] = same & (r[None, :] > r[:, None])
    us[nr:nr + FOX_HEADS] = (r[None, :] % FOX_HEADS) == np.arange(FOX_HEADS)[:, None]
    return ut.astype(np.float32), us.astype(np.float32)


def _fox_decode_kernel(pt_ref, qbd_ref, knb_ref, vrow_ref, lfn_ref, ut_ref, us_ref, *rest):
    npg = PAGES_PER_STEP
    kp = rest[0:npg]
    vp = rest[npg:2 * npg]
    lft_ref = rest[2 * npg]
    o_ref = rest[2 * npg + 1]
    m_sc, l_sc, acc_sc, carry_sc = rest[2 * npg + 2:]
    b = pl.program_id(0)
    j = pl.program_id(1)
    first_page = (pl.num_programs(1) - 1 - j) * npg
    nh = FOX_HEADS
    nr = npg * nh
    qbd = qbd_ref[0]

    def page2d(ref):
        return ref[...].reshape(FOX_W, PAGE_SIZE).astype(BF16)

    @pl.when(j == 0)
    def _():
        m_sc[...] = _dot(qbd, page2d(knb_ref.at[0]))
        lane = lax.broadcasted_iota(jnp.int32, (nh, LANES), 1)
        l_sc[...] = jnp.where(lane == 0, 1.0, 0.0)
        acc_sc[...] = jnp.broadcast_to(vrow_ref[0].astype(BF16).astype(F32), (nh, FOX_W))
        carry_sc[...] = lfn_ref[0]

    lfc = jnp.concatenate([lft_ref[pt_ref[b, first_page + g]] for g in range(npg)],
                          axis=0)
    w = _dot(jnp.concatenate(_split3(lfc), axis=0), ut_ref[...])

    s = [_dot(qbd, page2d(kp[g])) for g in range(npg)]

    wsum = w[0:nr] + w[nr:2 * nr] + w[2 * nr:3 * nr]
    x = _dot(us_ref[...], jnp.concatenate(_split3(wsum[:, LANES:]), axis=1))
    xs = x[:, :LANES] + x[:, LANES:2 * LANES] + x[:, 2 * LANES:]
    carry = carry_sc[...]
    sb = [s[g] + (wsum[g * nh:(g + 1) * nh, :LANES] + xs[g * nh:(g + 1) * nh] + carry)
          for g in range(npg)]
    carry_sc[...] = carry + xs[nr:nr + nh]

    mx = sb[0]
    for g in range(1, npg):
        mx = jnp.maximum(mx, sb[g])
    m_old = m_sc[...]
    m_new = jnp.maximum(m_old, jnp.max(mx, axis=1, keepdims=True))
    alpha = jnp.exp(m_old - m_new)
    p = [jnp.exp(sb[g] - m_new) for g in range(npg)]
    psum = p[0]
    for g in range(1, npg):
        psum = psum + p[g]
    l_sc[...] = alpha * l_sc[...] + psum
    m_sc[...] = m_new
    pv = _dot_nt(p[0].astype(BF16), page2d(vp[0]))
    for g in range(1, npg):
        pv = pv + _dot_nt(p[g].astype(BF16), page2d(vp[g]))
    acc_sc[...] = jnp.concatenate([alpha] * (FOX_W // LANES), axis=1) * acc_sc[...] + pv

    @pl.when(j == pl.num_programs(1) - 1)
    def _():
        ltot = jnp.sum(l_sc[...], axis=1, keepdims=True)
        own = (lax.broadcasted_iota(jnp.int32, (nh, FOX_W), 1) // FOX_DH
               == lax.broadcasted_iota(jnp.int32, (nh, FOX_W), 0))
        o_ref[0] = jnp.sum(jnp.where(own, acc_sc[...] / ltot, 0.0), axis=0, keepdims=True)


def _fox_decode(layer, page_table, qbd, knb, vrow, lfn, cache_kt, cache_vt, cache_lft):
    npg = PAGES_PER_STEP
    nsteps = N_PAGES // npg
    ut_np, us_np = _decode_tables()
    ut = jnp.asarray(ut_np, BF16)
    us = jnp.asarray(us_np, BF16)

    def page_map(g, tail):
        def f(b, j, pt):
            return (layer, pt[b, (nsteps - 1 - j) * npg + g]) + tail
        return f

    kv_specs = [pl.BlockSpec((None, None, FOX_HEADS, FOX_DH, PAGE_SIZE), page_map(g, (0, 0, 0)))
                for g in range(npg)]
    lf_spec = pl.BlockSpec((None,) + cache_lft.shape[1:], lambda b, j, pt: (layer, 0, 0, 0),
                           pipeline_mode=pl.Buffered(1))
    per_b = lambda shape: pl.BlockSpec((1,) + shape, lambda b, j, pt: (b,) + (0,) * len(shape))
    const = lambda shape: pl.BlockSpec(shape, lambda b, j, pt: (0,) * len(shape))
    hdl = (FOX_HEADS, FOX_DH, LANES)
    grid_spec = pltpu.PrefetchScalarGridSpec(
        num_scalar_prefetch=1,
        grid=(DEC_BATCH, nsteps),
        in_specs=[per_b((FOX_HEADS, FOX_W)), per_b(hdl), per_b((1, FOX_W)),
                  per_b((FOX_HEADS, LANES)), const(ut.shape), const(us.shape)]
        + kv_specs + kv_specs + [lf_spec],
        out_specs=per_b((1, FOX_W)),
        scratch_shapes=[pltpu.VMEM((FOX_HEADS, LANES), F32), pltpu.VMEM((FOX_HEADS, LANES), F32),
                        pltpu.VMEM((FOX_HEADS, FOX_W), F32), pltpu.VMEM((FOX_HEADS, LANES), F32)],
    )
    return pl.pallas_call(
        _fox_decode_kernel,
        grid_spec=grid_spec,
        out_shape=jax.ShapeDtypeStruct((DEC_BATCH, 1, FOX_W), F32),
        compiler_params=pltpu.CompilerParams(
            dimension_semantics=("parallel", "arbitrary"), vmem_limit_bytes=VMEM_LIMIT),
        name="fox_decode",
    )(page_table, qbd, knb, vrow, lfn, ut, us,
      *([cache_kt] * npg), *([cache_vt] * npg), cache_lft)


def _gla_decode_kernel(q_ref, k_ref, g_ref, v_ref, s_ref, o_ref, sn_ref):
    eye = (lax.broadcasted_iota(jnp.int32, (GLA_DK, GLA_DK), 0)
           == lax.broadcasted_iota(jnp.int32, (GLA_DK, GLA_DK), 1))

    def col(r):
        return jnp.sum(jnp.where(eye, jnp.broadcast_to(r, (GLA_DK, GLA_DK)), 0.0),
                       axis=1, keepdims=True)

    for r in range(GLA_DEC_ROWS):
        for h in range(GLA_HEADS):
            qc = col(q_ref[r, h:h + 1, :])
            kc = col(k_ref[r, h:h + 1, :])
            ac = col(jnp.exp(g_ref[r, h:h + 1, :]))
            sn = ac * s_ref[r, h] + kc * v_ref[r, h:h + 1, :]
            sn_ref[r, h] = sn
            o_ref[r, h:h + 1, :] = jnp.sum(qc * sn, axis=0, keepdims=True)


def _gla_decode(gq, gk, lg, gv, state):
    nb = GLA_DEC_ROWS
    hk = pl.BlockSpec((nb, GLA_HEADS, GLA_DK), lambda b: (b, 0, 0))
    hv = pl.BlockSpec((nb, GLA_HEADS, GLA_DV), lambda b: (b, 0, 0))
    st = pl.BlockSpec((nb, GLA_HEADS, GLA_DK, GLA_DV), lambda b: (b, 0, 0, 0))
    return pl.pallas_call(
        _gla_decode_kernel,
        grid=(DEC_BATCH // nb,),
        in_specs=[hk, hk, hk, hv, st],
        out_specs=(hv, st),
        out_shape=(jax.ShapeDtypeStruct((DEC_BATCH, GLA_HEADS, GLA_DV), F32),
                   jax.ShapeDtypeStruct((DEC_BATCH, GLA_HEADS, GLA_DK, GLA_DV), F32)),
        compiler_params=pltpu.CompilerParams(
            dimension_semantics=("parallel",), vmem_limit_bytes=VMEM_LIMIT),
        name="gla_decode",
    )(gq, gk, lg, gv, state)


def _stacked_weights(w_in, b_f, w_a2, b_a, gla_norm_g, w_pa, w_pb, w_o, ln1_g, ln1_b,
                     w_gate, w_up, w_down, ln2_g, ln2_b):
    o = _OFF
    sc_f = FOX_DH ** -0.5
    sc_g = GLA_DK ** -0.5
    seg = lambda a: w_in[:, :, o[a]:o[a + 1]]
    wqkv = jnp.concatenate([seg(0) * sc_f, seg(1), seg(2)], axis=2).astype(BF16)
    wgla = jnp.concatenate([seg(4) * sc_g, seg(5), seg(6)], axis=2).astype(BF16)
    ws = jnp.zeros((DEPTH, D_MODEL, LANES), F32)
    ws = ws.at[:, :, :FOX_HEADS].set(seg(3)).at[:, :, FOX_HEADS:FOX_HEADS + GLA_RANK].set(seg(8))
    w2 = jnp.zeros((DEPTH, LANES, GLA_K), F32).at[:, FOX_HEADS:FOX_HEADS + GLA_RANK].set(w_a2)
    bs = jnp.zeros((DEPTH, 1, LANES), F32).at[:, 0, :FOX_HEADS].set(b_f)
    wmg = jnp.concatenate([seg(7), seg(9), seg(10)], axis=2).astype(BF16)
    vec = lambda t: t[:, None, :]
    return dict(
        wqkv=wqkv, wgla=wgla, ws=ws.astype(BF16), w2=w2.astype(BF16), bs=bs, ba=vec(b_a),
        wmg=wmg, wpa=w_pa.astype(BF16), wpb=w_pb.astype(BF16), wo=w_o.astype(BF16),
        gn=vec(gla_norm_g), ln1g=vec(ln1_g), ln1b=vec(ln1_b),
        wgate=w_gate.astype(BF16), wup=w_up.astype(BF16), wdown=w_down.astype(BF16),
        ln2g=vec(ln2_g), ln2b=vec(ln2_b))


def kernel(x_prompt, x_sample, cache_k, cache_v, cache_lf, state_gla, page_table, meta, w_in, b_f,
           w_a2, b_a, gla_norm_g, w_pa, w_pb, w_o, ln1_g, ln1_b, w_gate, w_up, w_down, ln2_g, ln2_b):
    assert x_prompt.shape == (BATCH, SEQ, D_MODEL) and x_sample.shape == (DEC_BATCH, 1, D_MODEL)
    hp = jnp.concatenate([jnp.broadcast_to(meta.astype(F32)[None], (BATCH, N_META, D_MODEL)), x_prompt,
                          jnp.zeros((BATCH, PAD_ROWS, D_MODEL), F32)], axis=1)
    hs = x_sample.reshape(DEC_BATCH, D_MODEL)
    cache_kt = jnp.transpose(cache_k, (0, 1, 3, 4, 2))
    cache_vt = jnp.transpose(cache_v, (0, 1, 3, 4, 2))
    cache_lft = jnp.transpose(cache_lf, (0, 1, 3, 2))

    lw = _stacked_weights(w_in, b_f, w_a2, b_a, gla_norm_g, w_pa, w_pb, w_o, ln1_g, ln1_b,
                          w_gate, w_up, w_down, ln2_g, ln2_b)
    kp, vp, lfp, gp, ksr, vsr, lfs, gs = [], [], [], [], [], [], [], []
    for l in range(DEPTH):
        qa, ka, va, kf, vf, lf, gq, gk, gv, lg = _inproj(hp, lw, l)
        oa = _fox_prompt(qa, ka, va)
        ob, sfin = _gla_prompt(gq, gk, lg, gv)
        h2 = hp.reshape(BATCH * LP, D_MODEL)
        h2 = _merge(h2, oa.reshape(BATCH * LP, FOX_W), ob.reshape(BATCH * LP, GLA_V), lw, l, TM_TOK)
        h2 = _ffn(h2, lw, l, TM_TOK)
        hp = h2.reshape(BATCH, LP, D_MODEL)
        kp.append(kf)
        vp.append(vf)
        lfp.append(lf[:, :L_REAL])
        gp.append(sfin)
        r1, r2, lfs_full, lgs = _sample_inproj(hs, lw, l)
        q_s = r1[:, :FOX_W].reshape(DEC_BATCH, FOX_HEADS, FOX_DH)
        k_s = r1[:, FOX_W:2 * FOX_W].reshape(DEC_BATCH, FOX_HEADS, FOX_DH)
        v_s = r1[:, 2 * FOX_W:].reshape(DEC_BATCH, FOX_HEADS, FOX_DH)
        lf_s = lfs_full[:, :FOX_HEADS]
        lanes = lambda t: jnp.broadcast_to(t[..., None], t.shape + (LANES,))
        eye = jnp.eye(FOX_HEADS, dtype=F32)
        qbd = (q_s[:, :, None, :] * eye[None, :, :, None]).reshape(DEC_BATCH, FOX_HEADS, FOX_W)
        oa_s = _fox_decode(l, page_table, qbd.astype(BF16), lanes(k_s), r1[:, None, 2 * FOX_W:],
                           lanes(lf_s), cache_kt, cache_vt, cache_lft)
        ob_s, s_new = _gla_decode(r2[:, :GLA_K].reshape(DEC_BATCH, GLA_HEADS, GLA_DK),
                                  r2[:, GLA_K:2 * GLA_K].reshape(DEC_BATCH, GLA_HEADS, GLA_DK),
                                  lgs.reshape(DEC_BATCH, GLA_HEADS, GLA_DK),
                                  r2[:, 2 * GLA_K:].reshape(DEC_BATCH, GLA_HEADS, GLA_DV),
                                  state_gla[l])
        hs = _merge(hs, oa_s.reshape(DEC_BATCH, FOX_W).astype(BF16),
                    ob_s.reshape(DEC_BATCH, GLA_V), lw, l, DEC_BATCH)
        hs = _ffn(hs, lw, l, DEC_BATCH)
        ksr.append(k_s[:, None])
        vsr.append(v_s[:, None])
        lfs.append(lf_s[:, None])
        gs.append(s_new)

    y_prompt = hp[:, N_META:L_REAL]
    y_sample = hs[:, None, :]
    rows = lambda ts: jnp.transpose(
        jnp.stack(ts).reshape(DEPTH, BATCH, FOX_HEADS, FOX_DH, L_REAL), (0, 1, 4, 2, 3))
    return (y_prompt, y_sample, rows(kp), rows(vp), jnp.stack(lfp), jnp.stack(gp),
            jnp.stack(ksr), jnp.stack(vsr), jnp.stack(lfs), jnp.stack(gs))
```

```python
import functools

import numpy as np
import jax
import jax.numpy as jnp
from jax import lax
from jax.experimental import pallas as pl
from jax.experimental.pallas import tpu as pltpu

D_MODEL = 1024
BATCH = 2
SEQ = 8192
DEPTH = 2
DEC_BATCH = 32
PAST_LEN = 8192
PAGE_SIZE = 128
N_META = 16
FOX_HEADS = 8
FOX_DH = 64
FOX_W = FOX_HEADS * FOX_DH
GLA_HEADS = 4
GLA_DK = 64
GLA_DV = 128
GLA_K = GLA_HEADS * GLA_DK
GLA_V = GLA_HEADS * GLA_DV
GLA_RANK = 16
GLA_TAU = 16.0
D_FF = 2816
LN_EPS = 1e-5
NORM_EPS = 1e-6
NEG_INF = -1e30
ALPHA = (2.0 * DEPTH) ** 0.25
_SPLITS = (FOX_W, FOX_W, FOX_W, FOX_HEADS, GLA_K, GLA_K, GLA_V, GLA_V, GLA_RANK, D_MODEL, D_MODEL)
_OFF = np.concatenate([[0], np.cumsum(_SPLITS)]).tolist()

LANES = 128
SUBLANES = 8
VMEM_LIMIT = 56 * 1024 * 1024

L_REAL = SEQ + N_META
FOX_BLK = 256
FOX_BQ = 768
LOG2E = float(np.log2(np.e))
LP = -(-L_REAL // FOX_BLK) * FOX_BLK
PAD_ROWS = LP - L_REAL
GLA_CHUNK = 128
GLA_TILE = 768
TM_IN = 384
CUM_BLK = 128
TM_TOK = 512
N_PAGES = PAST_LEN // PAGE_SIZE
PAGES_PER_STEP = 16
GLA_DEC_ROWS = 8

F32 = jnp.float32
BF16 = jnp.bfloat16


def _dot(a, b):
    return jnp.dot(a, b, preferred_element_type=F32)


def _dot_nt(a, b):
    return lax.dot_general(a, b, (((1,), (1,)), ((), ())), preferred_element_type=F32)


def _dot_tn(a, b):
    return lax.dot_general(a, b, (((0,), (0,)), ((), ())), preferred_element_type=F32)


def _log_sigmoid(x):
    return jnp.minimum(x, 0.0) - jnp.log(1.0 + jnp.exp(-jnp.abs(x)))


def _sigmoid(x):
    return 1.0 / (1.0 + jnp.exp(-x))


def _layer_norm(y, g, b):
    mu = jnp.mean(y, axis=-1, keepdims=True)
    d = y - mu
    var = jnp.mean(d * d, axis=-1, keepdims=True)
    return d * lax.rsqrt(var + LN_EPS) * g + b


def _split3(x):
    hi = x.astype(BF16)
    r = x - hi.astype(F32)
    mid = r.astype(BF16)
    lo = (r - mid.astype(F32)).astype(BF16)
    return hi, mid, lo


def _fox_place_table():
    nh = FOX_HEADS
    t = np.zeros((LANES, 2 * nh * LANES), np.float32)
    for h in range(nh):
        xo = FOX_DH if h % 2 == 0 else 0
        qc = h * LANES + xo
        kc = (nh + h) * LANES + xo
        for part in range(3):
            t[part * nh + h, qc + part] = 1.0
            t[3 * nh, qc + 3 + part] = 1.0
            t[3 * nh, kc + part] = 1.0
            t[part * nh + h, kc + 3 + part] = -1.0
    return t


def _inproj_kernel(x_ref, wqkv_ref, wg_ref, ws_ref, w2_ref, bs_ref, ba_ref, tri_ref, place_ref,
                   q_ref, k_ref, v_ref, kf_ref, vf_ref, lf_ref, gq_ref, gk_ref, gv_ref, lg_ref,
                   carry_ref, *, tm):
    i = pl.program_id(1)

    @pl.when(i == 0)
    def _():
        carry_ref[...] = jnp.zeros_like(carry_ref)

    row = i * tm + lax.broadcasted_iota(jnp.int32, (tm, 1), 0)
    real = row < L_REAL
    xb = jnp.where(real, x_ref[0], 0.0).astype(BF16)

    rs = _dot(xb, ws_ref[...]) + bs_ref[...]
    lf_full = jnp.where(real, _log_sigmoid(rs), 0.0)
    lf_ref[0] = lf_full[:, :FOX_HEADS]
    z = _dot(rs.astype(BF16), w2_ref[...]) + ba_ref[...]
    lg_ref[0] = jnp.where(real, _log_sigmoid(z) * (1.0 / GLA_TAU), 0.0)

    carry = carry_ref[...]
    tri = tri_ref[...]
    cs = []
    for sb in range(tm // CUM_BLK):
        hi, mid, lo = _split3(lf_full[sb * CUM_BLK:(sb + 1) * CUM_BLK])
        c = _dot(tri, hi) + _dot(tri, mid) + _dot(tri, lo) + carry
        carry = c[CUM_BLK - 1:CUM_BLK]
        cs.append(c)
    carry_ref[...] = carry
    c = jnp.concatenate(cs, axis=0) * LOG2E

    r = _dot(xb, wqkv_ref[...])
    kf_ref[0] = r[:, FOX_W:2 * FOX_W].T
    vf_ref[0] = r[:, 2 * FOX_W:3 * FOX_W].T

    lane = lax.broadcasted_iota(jnp.int32, (1, LANES), 1)
    hi = c.astype(BF16).astype(F32)
    r1 = c - hi
    mid = r1.astype(BF16).astype(F32)
    lo = r1 - mid
    nh = FOX_HEADS
    parts = jnp.where(lane < nh, hi, jnp.where(lane < 2 * nh, pltpu.roll(mid, nh, axis=1), jnp.where(
        lane < 3 * nh, pltpu.roll(lo, 2 * nh, axis=1), jnp.where(lane == 3 * nh, 1.0, 0.0))))
    ext = _dot(parts.astype(BF16), place_ref[...])
    for h in range(FOX_HEADS):
        p, e = divmod(h, 2)
        dmask = (lane < FOX_DH) if e == 0 else (lane >= FOX_DH)
        xo = FOX_DH if e == 0 else 0
        ev = jnp.where(lane == xo, 1.0, 0.0)
        rq = r[:, p * LANES:(p + 1) * LANES]
        rk = r[:, FOX_W + p * LANES:FOX_W + (p + 1) * LANES]
        rv = r[:, 2 * FOX_W + p * LANES:2 * FOX_W + (p + 1) * LANES]
        q_ref[0, h] = jnp.where(dmask, rq * LOG2E, ext[:, h * LANES:(h + 1) * LANES]).astype(BF16)
        k_ref[0, h] = jnp.where(dmask, rk, ext[:, (nh + h) * LANES:(nh + h + 1) * LANES]).astype(BF16)
        vt = jnp.where(dmask, rv, ev)
        for cb in range(tm // LANES):
            v_ref[0, h, cb] = vt[cb * LANES:(cb + 1) * LANES].T.astype(BF16)

    rg = _dot(xb, wg_ref[...])
    gq_ref[0] = rg[:, :GLA_K]
    gk_ref[0] = rg[:, GLA_K:2 * GLA_K]
    gv_ref[0] = rg[:, 2 * GLA_K:].astype(BF16)


def _const_spec(shape):
    nd = len(shape)
    return pl.BlockSpec(shape, lambda *_: (0,) * nd, pipeline_mode=pl.Buffered(1))


def _layer_spec(w, l):
    nd = w.ndim - 1
    return pl.BlockSpec((None,) + w.shape[1:], lambda *_: (l,) + (0,) * nd,
                        pipeline_mode=pl.Buffered(1))


def _inproj(hp, lw, l):
    tm = TM_IN
    nt = LP // tm
    tri = jnp.asarray(np.tril(np.ones((CUM_BLK, CUM_BLK), np.float32)), BF16)
    place = jnp.asarray(_fox_place_table(), BF16)
    row3 = lambda w: pl.BlockSpec((1, tm, w), lambda b, i: (b, i, 0))
    head4 = pl.BlockSpec((1, FOX_HEADS, tm, LANES), lambda b, i: (b, 0, i, 0))
    col3 = pl.BlockSpec((1, FOX_W, tm), lambda b, i: (b, 0, i))
    vchunks = pl.BlockSpec((1, FOX_HEADS, tm // LANES, LANES, LANES), lambda b, i: (b, 0, i, 0, 0))
    out_shape = (
        jax.ShapeDtypeStruct((BATCH, FOX_HEADS, LP, LANES), BF16),
        jax.ShapeDtypeStruct((BATCH, FOX_HEADS, LP, LANES), BF16),
        jax.ShapeDtypeStruct((BATCH, FOX_HEADS, LP // LANES, LANES, LANES), BF16),
        jax.ShapeDtypeStruct((BATCH, FOX_W, L_REAL), F32),
        jax.ShapeDtypeStruct((BATCH, FOX_W, L_REAL), F32),
        jax.ShapeDtypeStruct((BATCH, LP, FOX_HEADS), F32),
        jax.ShapeDtypeStruct((BATCH, LP, GLA_K), F32),
        jax.ShapeDtypeStruct((BATCH, LP, GLA_K), F32),
        jax.ShapeDtypeStruct((BATCH, LP, GLA_V), BF16),
        jax.ShapeDtypeStruct((BATCH, LP, GLA_K), F32),
    )
    return pl.pallas_call(
        functools.partial(_inproj_kernel, tm=tm),
        grid=(BATCH, nt),
        in_specs=[row3(D_MODEL)]
        + [_layer_spec(lw[n], l) for n in ("wqkv", "wgla", "ws", "w2", "bs", "ba")]
        + [_const_spec(tri.shape), _const_spec(place.shape)],
        out_specs=(head4, head4, vchunks, col3, col3, row3(FOX_HEADS),
                   row3(GLA_K), row3(GLA_K), row3(GLA_V), row3(GLA_K)),
        out_shape=out_shape,
        scratch_shapes=[pltpu.VMEM((1, LANES), F32)],
        compiler_params=pltpu.CompilerParams(
            dimension_semantics=("parallel", "arbitrary"), vmem_limit_bytes=VMEM_LIMIT),
        name="prompt_inproj",
    )(hp, lw["wqkv"], lw["wgla"], lw["ws"], lw["w2"], lw["bs"], lw["ba"], tri, place)


def _fox_kernel(q_ref, k_ref, v_ref, o_ref, acc_sc, m_sc, mb_sc, s0_sc, s1_sc, s2_sc):
    i = pl.program_id(2)
    bq, bk = FOX_BQ, FOX_BLK
    nsub = bq // bk
    per_blk = bk // LANES

    for e in range(2):
        m_sc[e] = jnp.full((1, bq), NEG_INF, F32)
        acc_sc[e] = jnp.zeros((LANES, bq), F32)

    def values_t(e, j):
        return jnp.concatenate([v_ref[0, e, per_blk * j + s] for s in range(per_blk)], axis=1)

    sbuf = (s0_sc, s1_sc, s2_sc)
    causal = (lax.broadcasted_iota(jnp.int32, (bk, 1), 0)
              <= lax.broadcasted_iota(jnp.int32, (1, bk), 1))

    def scores(j, slot, c0):
        start = j * bk if isinstance(j, int) else pl.multiple_of(j * bk, bk)
        for e in range(2):
            st = _dot_nt(k_ref[0, e, pl.ds(start, bk), :], q_ref[0, e, c0:, :])
            sbuf[slot][e, :, c0:] = st
            mb_sc[slot, e, :, c0:] = jnp.max(st, axis=0, keepdims=True)

    def step(j, cur, c0, masked, next_c0):
        if next_c0 is not None:
            scores(j + 1, (cur + 1) % nsub, next_c0)
        for e in range(2):
            for sb in range((bq - c0) // bk):
                cols = slice(c0 + sb * bk, c0 + (sb + 1) * bk)
                st = sbuf[cur][e, :, cols]
                if masked and sb == 0:
                    st = jnp.where(causal, st, NEG_INF)
                    mblk = jnp.max(st, axis=0, keepdims=True)
                else:
                    mblk = mb_sc[cur, e, :, cols]
                m_old = m_sc[e, :, cols]
                m_new = jnp.maximum(m_old, mblk)
                alpha = jnp.exp2(m_old - m_new)
                pt = jnp.exp2(st - m_new).astype(BF16)
                acc_sc[e, :, cols] = alpha * acc_sc[e, :, cols] + _dot(values_t(e, j), pt)
                m_sc[e, :, cols] = m_new

    scores(0, 0, 0)

    def body(ii, carry):
        for s in range(nsub):
            step(nsub * ii + s, s, 0, False, 0)
        return carry
    lax.fori_loop(0, i, body, 0)

    for s in range(nsub):
        step(nsub * i + s, s, s * bk, True, (s + 1) * bk if s + 1 < nsub else None)

    a0 = acc_sc[0]
    a1 = acc_sc[1]
    row = lax.broadcasted_iota(jnp.int32, (LANES, 1), 0)
    ot = jnp.where(row < FOX_DH, a0 / a0[FOX_DH:FOX_DH + 1, :], a1 / a1[0:1, :])
    o_ref[0] = ot.T.astype(BF16)


def _fox_prompt(q, k, v):
    nq = LP // FOX_BQ
    score_buf = pltpu.VMEM((2, FOX_BLK, FOX_BQ), F32)
    return pl.pallas_call(
        _fox_kernel,
        grid=(BATCH, FOX_HEADS // 2, nq),
        in_specs=[pl.BlockSpec((1, 2, FOX_BQ, LANES), lambda b, p, i: (b, p, i, 0)),
                  pl.BlockSpec((1, 2, LP, LANES), lambda b, p, i: (b, p, 0, 0)),
                  pl.BlockSpec((1, 2, LP // LANES, LANES, LANES), lambda b, p, i: (b, p, 0, 0, 0))],
        out_specs=pl.BlockSpec((1, FOX_BQ, LANES), lambda b, p, i: (b, i, p)),
        out_shape=jax.ShapeDtypeStruct((BATCH, LP, FOX_W), BF16),
        scratch_shapes=[pltpu.VMEM((2, LANES, FOX_BQ), F32),
                        pltpu.VMEM((2, 1, FOX_BQ), F32),
                        pltpu.VMEM((FOX_BQ // FOX_BLK, 2, 1, FOX_BQ), F32),
                        score_buf, score_buf, score_buf],
        compiler_params=pltpu.CompilerParams(
            dimension_semantics=("parallel", "parallel", "arbitrary"), vmem_limit_bytes=VMEM_LIMIT),
        name="fox_prompt",
    )(q, k, v)


def _gla_tables():
    c = GLA_CHUNK
    t = np.arange(c)[:, None]
    j = np.arange(c)[None, :]
    mats = [(j <= t).astype(np.float32), (j > t).astype(np.float32)]
    masks = [np.eye(c, dtype=np.float32)]
    blk = c
    while blk >= 2:
        half = blk // 2
        mid = (t // blk) * blk + half
        mats.append((j <= t).astype(np.float32) - (j <= mid).astype(np.float32))
        s = j
        masks.append((((t // blk) == (s // blk)) & ((t % blk) >= half) & ((s % blk) < half))
                     .astype(np.float32))
        blk = half
    return np.concatenate(mats, axis=0), np.stack([np.concatenate([m, m], axis=0) for m in masks])


_GLA_LEVELS = 7


def _gla_kernel(q_ref, k_ref, lg_ref, v_ref, dall_ref, masks_ref, o_ref, sfin_ref, st_sc):
    i = pl.program_id(1)
    c = GLA_CHUNK
    npair = GLA_HEADS // 2

    @pl.when(i == 0)
    def _():
        st_sc[...] = jnp.zeros_like(st_sc)

    lane = lax.broadcasted_iota(jnp.int32, (1, LANES), 1)
    hmask = (lane < GLA_DK, lane >= GLA_DK)

    for ci in range(GLA_TILE // c):
        rows = slice(ci * c, (ci + 1) * c)
        for pr in range(npair):
            pl_ = slice(pr * LANES, (pr + 1) * LANES)
            q = q_ref[0, rows, pl_]
            k = k_ref[0, rows, pl_]
            lg = lg_ref[0, rows, pl_]
            hi = lg.astype(BF16)
            lo = (lg - hi.astype(F32)).astype(BF16)
            e2 = _dot(dall_ref[...], jnp.concatenate([hi, lo], axis=1))
            ex = e2[:, :LANES] + e2[:, LANES:]
            bc = ex[0:c]
            q_in = q * jnp.exp(bc)
            k_dec = k * jnp.exp(ex[c:2 * c])
            qs = [q]
            ks = [k.astype(BF16)]
            for lv in range(_GLA_LEVELS):
                f = jnp.exp(-jnp.abs(ex[(2 + lv) * c:(3 + lv) * c]))
                qs.append(q * f)
                ks.append((k * f).astype(BF16))
            decay_all = jnp.exp(bc[c - 1:c])
            a2 = jnp.zeros((2 * c, c), F32)
            for lv in range(_GLA_LEVELS + 1):
                ql = jnp.concatenate([jnp.where(hmask[0], qs[lv], 0.0),
                                      jnp.where(hmask[1], qs[lv], 0.0)], axis=0).astype(BF16)
                a2 = a2 + masks_ref[lv] * _dot_nt(ql, ks[lv])
            for h in range(2):
                hh = 2 * pr + h
                a = a2[h * c:(h + 1) * c]
                vh = v_ref[0, rows, hh * GLA_DV:(hh + 1) * GLA_DV]
                st = st_sc[hh]
                o = _dot(a.astype(BF16), vh) + _dot_nt(
                    jnp.where(hmask[h], q_in, 0.0).astype(BF16), st.astype(BF16))
                o_ref[0, rows, hh * GLA_DV:(hh + 1) * GLA_DV] = o
                kd = jnp.where(hmask[h], k_dec, 0.0).astype(BF16)
                st_sc[hh] = decay_all * st + _dot_tn(vh, kd)

    @pl.when(i == pl.num_programs(1) - 1)
    def _():
        for hh in range(GLA_HEADS):
            s = st_sc[hh].T
            h = hh % 2
            sfin_ref[0, hh] = s[h * GLA_DK:(h + 1) * GLA_DK, :]


def _gla_prompt(gq, gk, lg, gv):
    dall_np, masks_np = _gla_tables()
    dall = jnp.asarray(dall_np, BF16)
    masks = jnp.asarray(masks_np, F32)
    t = GLA_TILE
    qk = pl.BlockSpec((1, t, GLA_K), lambda b, i: (b, i, 0))
    wide = pl.BlockSpec((1, t, GLA_V), lambda b, i: (b, i, 0))
    return pl.pallas_call(
        _gla_kernel,
        grid=(BATCH, LP // t),
        in_specs=[qk, qk, qk, wide, _const_spec(dall.shape), _const_spec(masks.shape)],
        out_specs=(wide, pl.BlockSpec((1, GLA_HEADS, GLA_DK, GLA_DV), lambda b, i: (b, 0, 0, 0))),
        out_shape=(jax.ShapeDtypeStruct((BATCH, LP, GLA_V), F32),
                   jax.ShapeDtypeStruct((BATCH, GLA_HEADS, GLA_DK, GLA_DV), F32)),
        scratch_shapes=[pltpu.VMEM((GLA_HEADS, GLA_DV, LANES), F32)],
        compiler_params=pltpu.CompilerParams(
            dimension_semantics=("parallel", "arbitrary"), vmem_limit_bytes=VMEM_LIMIT),
        name="gla_prompt",
    )(gq, gk, lg, gv, dall, masks)


def _merge_kernel(h_ref, oa_ref, ob_ref, wg_ref, wpa_ref, wpb_ref, wo_ref, gn_ref, g_ref, b_ref,
                  out_ref):
    x = h_ref[...]
    xb = x.astype(BF16)
    r = _dot(xb, wg_ref[...])
    rb = r[:, :GLA_V]
    ga = r[:, GLA_V:GLA_V + D_MODEL]
    gb = r[:, GLA_V + D_MODEL:]
    ob = ob_ref[...]
    parts = []
    for hd in range(GLA_HEADS):
        o = ob[:, hd * GLA_DV:(hd + 1) * GLA_DV]
        ms = jnp.mean(o * o, axis=-1, keepdims=True)
        parts.append(o * lax.rsqrt(ms + NORM_EPS) * gn_ref[...])
    obn = jnp.concatenate(parts, axis=1) * (rb * _sigmoid(rb))
    y_a = _dot(oa_ref[...], wpa_ref[...])
    y_b = _dot(obn.astype(BF16), wpb_ref[...])
    mixed = _sigmoid(ga) * y_a + _sigmoid(gb) * y_b
    y = ALPHA * x + _dot(mixed.astype(BF16), wo_ref[...])
    out_ref[...] = _layer_norm(y, g_ref[...], b_ref[...])


def _merge(h, oa, ob, lw, l, tm):
    m = h.shape[0]
    row = lambda w: pl.BlockSpec((tm, w), lambda i: (i, 0))
    ws = (lw["wmg"], lw["wpa"], lw["wpb"], lw["wo"], lw["gn"], lw["ln1g"], lw["ln1b"])
    return pl.pallas_call(
        _merge_kernel,
        grid=(m // tm,),
        in_specs=[row(D_MODEL), row(FOX_W), row(GLA_V)] + [_layer_spec(w, l) for w in ws],
        out_specs=row(D_MODEL),
        out_shape=jax.ShapeDtypeStruct((m, D_MODEL), F32),
        compiler_params=pltpu.CompilerParams(
            dimension_semantics=("parallel",), vmem_limit_bytes=VMEM_LIMIT),
        name="merge",
    )(h, oa, ob, *ws)


def _ffn_kernel(h_ref, wgate_ref, wup_ref, wdown_ref, g_ref, b_ref, out_ref):
    x = h_ref[...]
    xb = x.astype(BF16)
    gt = _dot(xb, wgate_ref[...])
    up = _dot(xb, wup_ref[...])
    hdn = (gt * _sigmoid(gt) * up).astype(BF16)
    y = ALPHA * x + _dot(hdn, wdown_ref[...])
    out_ref[...] = _layer_norm(y, g_ref[...], b_ref[...])


def _ffn(h, lw, l, tm):
    m = h.shape[0]
    row = pl.BlockSpec((tm, D_MODEL), lambda i: (i, 0))
    ws = (lw["wgate"], lw["wup"], lw["wdown"], lw["ln2g"], lw["ln2b"])
    return pl.pallas_call(
        _ffn_kernel,
        grid=(m // tm,),
        in_specs=[row] + [_layer_spec(w, l) for w in ws],
        out_specs=row,
        out_shape=jax.ShapeDtypeStruct((m, D_MODEL), F32),
        compiler_params=pltpu.CompilerParams(
            dimension_semantics=("parallel",), vmem_limit_bytes=VMEM_LIMIT),
        name="ffn",
    )(h, *ws)


def _sample_inproj_kernel(x_ref, wqkv_ref, wg_ref, ws_ref, w2_ref, bs_ref, ba_ref,
                          r1_ref, r2_ref, lf_ref, lg_ref):
    xb = x_ref[...].astype(BF16)
    r1_ref[...] = _dot(xb, wqkv_ref[...])
    r2_ref[...] = _dot(xb, wg_ref[...])
    rs = _dot(xb, ws_ref[...]) + bs_ref[...]
    lf_ref[...] = _log_sigmoid(rs)
    z = _dot(rs.astype(BF16), w2_ref[...]) + ba_ref[...]
    lg_ref[...] = _log_sigmoid(z) * (1.0 / GLA_TAU)


def _sample_inproj(x, lw, l):
    ws = (lw["wqkv"], lw["wgla"], lw["ws"], lw["w2"], lw["bs"], lw["ba"])
    full = lambda shape: pl.BlockSpec(shape, lambda i: (0,) * len(shape))
    out_shape = (jax.ShapeDtypeStruct((DEC_BATCH, 3 * FOX_W), F32),
                 jax.ShapeDtypeStruct((DEC_BATCH, 2 * GLA_K + GLA_V), F32),
                 jax.ShapeDtypeStruct((DEC_BATCH, LANES), F32),
                 jax.ShapeDtypeStruct((DEC_BATCH, GLA_K), F32))
    return pl.pallas_call(
        _sample_inproj_kernel,
        grid=(1,),
        in_specs=[full(x.shape)] + [_layer_spec(w, l) for w in ws],
        out_specs=tuple(full(s.shape) for s in out_shape),
        out_shape=out_shape,
        compiler_params=pltpu.CompilerParams(vmem_limit_bytes=VMEM_LIMIT),
        name="sample_inproj",
    )(x, *ws)


def _decode_tables():
    j = np.arange(LANES)
    ut = np.concatenate([j[:, None] > j[None, :], np.ones((LANES, LANES), bool)], axis=1)
    nr = PAGES_PER_STEP * FOX_HEADS
    r = np.arange(nr)
    same = (r[:, None] % FOX_HEADS) == (r[None, :] % FOX_HEADS)
    us = np.zeros((nr + 2 * SUBLANES, nr), bool)
    us[:nr] = same & (r[None, :] > r[:, None])
    us[nr:nr + FOX_HEADS] = (r[None, :] % FOX_HEADS) == np.arange(FOX_HEADS)[:, None]
    return ut.astype(np.float32), us.astype(np.float32)


def _fox_decode_kernel(pt_ref, qbd_ref, knb_ref, vrow_ref, lfn_ref, ut_ref, us_ref, *rest):
    npg = PAGES_PER_STEP
    kp = rest[0:npg]
    vp = rest[npg:2 * npg]
    lft_ref = rest[2 * npg]
    o_ref = rest[2 * npg + 1]
    m_sc, l_sc, acc_sc, carry_sc = rest[2 * npg + 2:]
    b = pl.program_id(0)
    j = pl.program_id(1)
    first_page = (pl.num_programs(1) - 1 - j) * npg
    nh = FOX_HEADS
    nr = npg * nh
    qbd = qbd_ref[0]

    def page2d(ref):
        return ref[...].reshape(FOX_W, PAGE_SIZE).astype(BF16)

    @pl.when(j == 0)
    def _():
        m_sc[...] = _dot(qbd, page2d(knb_ref.at[0]))
        lane = lax.broadcasted_iota(jnp.int32, (nh, LANES), 1)
        l_sc[...] = jnp.where(lane == 0, 1.0, 0.0)
        acc_sc[...] = jnp.broadcast_to(vrow_ref[0].astype(BF16).astype(F32), (nh, FOX_W))
        carry_sc[...] = lfn_ref[0]

    lfc = jnp.concatenate([lft_ref[pt_ref[b, first_page + g]] for g in range(npg)],
                          axis=0)
    w = _dot(jnp.concatenate(_split3(lfc), axis=0), ut_ref[...])

    s = [_dot(qbd, page2d(kp[g])) for g in range(npg)]

    wsum = w[0:nr] + w[nr:2 * nr] + w[2 * nr:3 * nr]
    x = _dot(us_ref[...], jnp.concatenate(_split3(wsum[:, LANES:]), axis=1))
    xs = x[:, :LANES] + x[:, LANES:2 * LANES] + x[:, 2 * LANES:]
    carry = carry_sc[...]
    sb = [s[g] + (wsum[g * nh:(g + 1) * nh, :LANES] + xs[g * nh:(g + 1) * nh] + carry)
          for g in range(npg)]
    carry_sc[...] = carry + xs[nr:nr + nh]

    mx = sb[0]
    for g in range(1, npg):
        mx = jnp.maximum(mx, sb[g])
    m_old = m_sc[...]
    m_new = jnp.maximum(m_old, jnp.max(mx, axis=1, keepdims=True))
    alpha = jnp.exp(m_old - m_new)
    p = [jnp.exp(sb[g] - m_new) for g in range(npg)]
    psum = p[0]
    for g in range(1, npg):
        psum = psum + p[g]
    l_sc[...] = alpha * l_sc[...] + psum
    m_sc[...] = m_new
    pv = _dot_nt(p[0].astype(BF16), page2d(vp[0]))
    for g in range(1, npg):
        pv = pv + _dot_nt(p[g].astype(BF16), page2d(vp[g]))
    acc_sc[...] = jnp.concatenate([alpha] * (FOX_W // LANES), axis=1) * acc_sc[...] + pv

    @pl.when(j == pl.num_programs(1) - 1)
    def _():
        ltot = jnp.sum(l_sc[...], axis=1, keepdims=True)
        own = (lax.broadcasted_iota(jnp.int32, (nh, FOX_W), 1) // FOX_DH
               == lax.broadcasted_iota(jnp.int32, (nh, FOX_W), 0))
        o_ref[0] = jnp.sum(jnp.where(own, acc_sc[...] / ltot, 0.0), axis=0, keepdims=True)


def _fox_decode(layer, page_table, qbd, knb, vrow, lfn, cache_kt, cache_vt, cache_lft):
    npg = PAGES_PER_STEP
    nsteps = N_PAGES // npg
    ut_np, us_np = _decode_tables()
    ut = jnp.asarray(ut_np, BF16)
    us = jnp.asarray(us_np, BF16)

    def page_map(g, tail):
        def f(b, j, pt):
            return (layer, pt[b, (nsteps - 1 - j) * npg + g]) + tail
        return f

    kv_specs = [pl.BlockSpec((None, None, FOX_HEADS, FOX_DH, PAGE_SIZE), page_map(g, (0, 0, 0)))
                for g in range(npg)]
    lf_spec = pl.BlockSpec((None,) + cache_lft.shape[1:], lambda b, j, pt: (layer, 0, 0, 0),
                           pipeline_mode=pl.Buffered(1))
    per_b = lambda shape: pl.BlockSpec((1,) + shape, lambda b, j, pt: (b,) + (0,) * len(shape))
    const = lambda shape: pl.BlockSpec(shape, lambda b, j, pt: (0,) * len(shape))
    hdl = (FOX_HEADS, FOX_DH, LANES)
    grid_spec = pltpu.PrefetchScalarGridSpec(
        num_scalar_prefetch=1,
        grid=(DEC_BATCH, nsteps),
        in_specs=[per_b((FOX_HEADS, FOX_W)), per_b(hdl), per_b((1, FOX_W)),
                  per_b((FOX_HEADS, LANES)), const(ut.shape), const(us.shape)]
        + kv_specs + kv_specs + [lf_spec],
        out_specs=per_b((1, FOX_W)),
        scratch_shapes=[pltpu.VMEM((FOX_HEADS, LANES), F32), pltpu.VMEM((FOX_HEADS, LANES), F32),
                        pltpu.VMEM((FOX_HEADS, FOX_W), F32), pltpu.VMEM((FOX_HEADS, LANES), F32)],
    )
    return pl.pallas_call(
        _fox_decode_kernel,
        grid_spec=grid_spec,
        out_shape=jax.ShapeDtypeStruct((DEC_BATCH, 1, FOX_W), F32),
        compiler_params=pltpu.CompilerParams(
            dimension_semantics=("parallel", "arbitrary"), vmem_limit_bytes=VMEM_LIMIT),
        name="fox_decode",
    )(page_table, qbd, knb, vrow, lfn, ut, us,
      *([cache_kt] * npg), *([cache_vt] * npg), cache_lft)


def _gla_decode_kernel(q_ref, k_ref, g_ref, v_ref, s_ref, o_ref, sn_ref):
    eye = (lax.broadcasted_iota(jnp.int32, (GLA_DK, GLA_DK), 0)
           == lax.broadcasted_iota(jnp.int32, (GLA_DK, GLA_DK), 1))

    def col(r):
        return jnp.sum(jnp.where(eye, jnp.broadcast_to(r, (GLA_DK, GLA_DK)), 0.0),
                       axis=1, keepdims=True)

    for r in range(GLA_DEC_ROWS):
        for h in range(GLA_HEADS):
            qc = col(q_ref[r, h:h + 1, :])
            kc = col(k_ref[r, h:h + 1, :])
            ac = col(jnp.exp(g_ref[r, h:h + 1, :]))
            sn = ac * s_ref[r, h] + kc * v_ref[r, h:h + 1, :]
            sn_ref[r, h] = sn
            o_ref[r, h:h + 1, :] = jnp.sum(qc * sn, axis=0, keepdims=True)


def _gla_decode(gq, gk, lg, gv, state):
    nb = GLA_DEC_ROWS
    hk = pl.BlockSpec((nb, GLA_HEADS, GLA_DK), lambda b: (b, 0, 0))
    hv = pl.BlockSpec((nb, GLA_HEADS, GLA_DV), lambda b: (b, 0, 0))
    st = pl.BlockSpec((nb, GLA_HEADS, GLA_DK, GLA_DV), lambda b: (b, 0, 0, 0))
    return pl.pallas_call(
        _gla_decode_kernel,
        grid=(DEC_BATCH // nb,),
        in_specs=[hk, hk, hk, hv, st],
        out_specs=(hv, st),
        out_shape=(jax.ShapeDtypeStruct((DEC_BATCH, GLA_HEADS, GLA_DV), F32),
                   jax.ShapeDtypeStruct((DEC_BATCH, GLA_HEADS, GLA_DK, GLA_DV), F32)),
        compiler_params=pltpu.CompilerParams(
            dimension_semantics=("parallel",), vmem_limit_bytes=VMEM_LIMIT),
        name="gla_decode",
    )(gq, gk, lg, gv, state)


def _stacked_weights(w_in, b_f, w_a2, b_a, gla_norm_g, w_pa, w_pb, w_o, ln1_g, ln1_b,
                     w_gate, w_up, w_down, ln2_g, ln2_b):
    o = _OFF
    sc_f = FOX_DH ** -0.5
    sc_g = GLA_DK ** -0.5
    seg = lambda a: w_in[:, :, o[a]:o[a + 1]]
    wqkv = jnp.concatenate([seg(0) * sc_f, seg(1), seg(2)], axis=2).astype(BF16)
    wgla = jnp.concatenate([seg(4) * sc_g, seg(5), seg(6)], axis=2).astype(BF16)
    ws = jnp.zeros((DEPTH, D_MODEL, LANES), F32)
    ws = ws.at[:, :, :FOX_HEADS].set(seg(3)).at[:, :, FOX_HEADS:FOX_HEADS + GLA_RANK].set(seg(8))
    w2 = jnp.zeros((DEPTH, LANES, GLA_K), F32).at[:, FOX_HEADS:FOX_HEADS + GLA_RANK].set(w_a2)
    bs = jnp.zeros((DEPTH, 1, LANES), F32).at[:, 0, :FOX_HEADS].set(b_f)
    wmg = jnp.concatenate([seg(7), seg(9), seg(10)], axis=2).astype(BF16)
    vec = lambda t: t[:, None, :]
    return dict(
        wqkv=wqkv, wgla=wgla, ws=ws.astype(BF16), w2=w2.astype(BF16), bs=bs, ba=vec(b_a),
        wmg=wmg, wpa=w_pa.astype(BF16), wpb=w_pb.astype(BF16), wo=w_o.astype(BF16),
        gn=vec(gla_norm_g), ln1g=vec(ln1_g), ln1b=vec(ln1_b),
        wgate=w_gate.astype(BF16), wup=w_up.astype(BF16), wdown=w_down.astype(BF16),
        ln2g=vec(ln2_g), ln2b=vec(ln2_b))


def kernel(x_prompt, x_sample, cache_k, cache_v, cache_lf, state_gla, page_table, meta, w_in, b_f,
           w_a2, b_a, gla_norm_g, w_pa, w_pb, w_o, ln1_g, ln1_b, w_gate, w_up, w_down, ln2_g, ln2_b):
    assert x_prompt.shape == (BATCH, SEQ, D_MODEL) and x_sample.shape == (DEC_BATCH, 1, D_MODEL)
    hp = jnp.concatenate([jnp.broadcast_to(meta.astype(F32)[None], (BATCH, N_META, D_MODEL)), x_prompt,
                          jnp.zeros((BATCH, PAD_ROWS, D_MODEL), F32)], axis=1)
    hs = x_sample.reshape(DEC_BATCH, D_MODEL)
    cache_kt = jnp.transpose(cache_k, (0, 1, 3, 4, 2))
    cache_vt = jnp.transpose(cache_v, (0, 1, 3, 4, 2))
    cache_lft = jnp.transpose(cache_lf, (0, 1, 3, 2))

    lw = _stacked_weights(w_in, b_f, w_a2, b_a, gla_norm_g, w_pa, w_pb, w_o, ln1_g, ln1_b,
                          w_gate, w_up, w_down, ln2_g, ln2_b)
    kp, vp, lfp, gp, ksr, vsr, lfs, gs = [], [], [], [], [], [], [], []
    for l in range(DEPTH):
        qa, ka, va, kf, vf, lf, gq, gk, gv, lg = _inproj(hp, lw, l)
        oa = _fox_prompt(qa, ka, va)
        ob, sfin = _gla_prompt(gq, gk, lg, gv)
        h2 = hp.reshape(BATCH * LP, D_MODEL)
        h2 = _merge(h2, oa.reshape(BATCH * LP, FOX_W), ob.reshape(BATCH * LP, GLA_V), lw, l, TM_TOK)
        h2 = _ffn(h2, lw, l, TM_TOK)
        hp = h2.reshape(BATCH, LP, D_MODEL)
        kp.append(kf)
        vp.append(vf)
        lfp.append(lf[:, :L_REAL])
        gp.append(sfin)
        r1, r2, lfs_full, lgs = _sample_inproj(hs, lw, l)
        q_s = r1[:, :FOX_W].reshape(DEC_BATCH, FOX_HEADS, FOX_DH)
        k_s = r1[:, FOX_W:2 * FOX_W].reshape(DEC_BATCH, FOX_HEADS, FOX_DH)
        v_s = r1[:, 2 * FOX_W:].reshape(DEC_BATCH, FOX_HEADS, FOX_DH)
        lf_s = lfs_full[:, :FOX_HEADS]
        lanes = lambda t: jnp.broadcast_to(t[..., None], t.shape + (LANES,))
        eye = jnp.eye(FOX_HEADS, dtype=F32)
        qbd = (q_s[:, :, None, :] * eye[None, :, :, None]).reshape(DEC_BATCH, FOX_HEADS, FOX_W)
        oa_s = _fox_decode(l, page_table, qbd.astype(BF16), lanes(k_s), r1[:, None, 2 * FOX_W:],
                           lanes(lf_s), cache_kt, cache_vt, cache_lft)
        ob_s, s_new = _gla_decode(r2[:, :GLA_K].reshape(DEC_BATCH, GLA_HEADS, GLA_DK),
                                  r2[:, GLA_K:2 * GLA_K].reshape(DEC_BATCH, GLA_HEADS, GLA_DK),
                                  lgs.reshape(DEC_BATCH, GLA_HEADS, GLA_DK),
                                  r2[:, 2 * GLA_K:].reshape(DEC_BATCH, GLA_HEADS, GLA_DV),
                                  state_gla[l])
        hs = _merge(hs, oa_s.reshape(DEC_BATCH, FOX_W).astype(BF16),
                    ob_s.reshape(DEC_BATCH, GLA_V), lw, l, DEC_BATCH)
        hs = _ffn(hs, lw, l, DEC_BATCH)
        ksr.append(k_s[:, None])
        vsr.append(v_s[:, None])
        lfs.append(lf_s[:, None])
        gs.append(s_new)

    y_prompt = hp[:, N_META:L_REAL]
    y_sample = hs[:, None, :]
    rows = lambda ts: jnp.transpose(
        jnp.stack(ts).reshape(DEPTH, BATCH, FOX_HEADS, FOX_DH, L_REAL), (0, 1, 4, 2, 3))
    return (y_prompt, y_sample, rows(kp), rows(vp), jnp.stack(lfp), jnp.stack(gp),
            jnp.stack(ksr), jnp.stack(vsr), jnp.stack(lfs), jnp.stack(gs))
```

```python
import functools

import numpy as np
import jax
import jax.numpy as jnp
from jax import lax
from jax.experimental import pallas as pl
from jax.experimental.pallas import tpu as pltpu

D_MODEL = 1024
BATCH = 2
SEQ = 8192
DEPTH = 2
DEC_BATCH = 32
PAST_LEN = 8192
PAGE_SIZE = 128
N_META = 16
FOX_HEADS = 8
FOX_DH = 64
FOX_W = FOX_HEADS * FOX_DH
GLA_HEADS = 4
GLA_DK = 64
GLA_DV = 128
GLA_K = GLA_HEADS * GLA_DK
GLA_V = GLA_HEADS * GLA_DV
GLA_RANK = 16
GLA_TAU = 16.0
D_FF = 2816
LN_EPS = 1e-5
NORM_EPS = 1e-6
NEG_INF = -1e30
ALPHA = (2.0 * DEPTH) ** 0.25
_SPLITS = (FOX_W, FOX_W, FOX_W, FOX_HEADS, GLA_K, GLA_K, GLA_V, GLA_V, GLA_RANK, D_MODEL, D_MODEL)
_OFF = np.concatenate([[0], np.cumsum(_SPLITS)]).tolist()

LANES = 128
SUBLANES = 8
VMEM_LIMIT = 56 * 1024 * 1024

L_REAL = SEQ + N_META
FOX_BLK = 256
FOX_BQ = 768
LOG2E = float(np.log2(np.e))
LP = -(-L_REAL // FOX_BLK) * FOX_BLK
PAD_ROWS = LP - L_REAL
GLA_CHUNK = 128
GLA_TILE = 768
TM_IN = 384
CUM_BLK = 128
TM_TOK = 512
N_PAGES = PAST_LEN // PAGE_SIZE
PAGES_PER_STEP = 16
GLA_DEC_ROWS = 8

F32 = jnp.float32
BF16 = jnp.bfloat16


def _dot(a, b):
    return jnp.dot(a, b, preferred_element_type=F32)


def _dot_nt(a, b):
    return lax.dot_general(a, b, (((1,), (1,)), ((), ())), preferred_element_type=F32)


def _dot_tn(a, b):
    return lax.dot_general(a, b, (((0,), (0,)), ((), ())), preferred_element_type=F32)


def _log_sigmoid(x):
    return jnp.minimum(x, 0.0) - jnp.log(1.0 + jnp.exp(-jnp.abs(x)))


def _sigmoid(x):
    return 1.0 / (1.0 + jnp.exp(-x))


def _layer_norm(y, g, b):
    mu = jnp.mean(y, axis=-1, keepdims=True)
    d = y - mu
    var = jnp.mean(d * d, axis=-1, keepdims=True)
    return d * lax.rsqrt(var + LN_EPS) * g + b


def _split3(x):
    hi = x.astype(BF16)
    r = x - hi.astype(F32)
    mid = r.astype(BF16)
    lo = (r - mid.astype(F32)).astype(BF16)
    return hi, mid, lo


def _fox_place_table():
    nh = FOX_HEADS
    t = np.zeros((LANES, 2 * nh * LANES), np.float32)
    for h in range(nh):
        xo = FOX_DH if h % 2 == 0 else 0
        qc = h * LANES + xo
        kc = (nh + h) * LANES + xo
        for part in range(3):
            t[part * nh + h, qc + part] = 1.0
            t[3 * nh, qc + 3 + part] = 1.0
            t[3 * nh, kc + part] = 1.0
            t[part * nh + h, kc + 3 + part] = -1.0
    return t


def _inproj_kernel(x_ref, wqkv_ref, wg_ref, ws_ref, w2_ref, bs_ref, ba_ref, tri_ref, place_ref,
                   *rest, tm, n_aliased):
    (q_ref, k_ref, v_ref, kf_ref, vf_ref, lf_ref, gq_ref, gk_ref, gv_ref, lg_ref,
     carry_ref) = rest[n_aliased:]
    i = pl.program_id(1)

    @pl.when(i == 0)
    def _():
        carry_ref[...] = jnp.zeros_like(carry_ref)

    row = i * tm + lax.broadcasted_iota(jnp.int32, (tm, 1), 0)
    real = row < L_REAL
    xb = jnp.where(real, x_ref[0], 0.0).astype(BF16)

    rs = _dot(xb, ws_ref[...]) + bs_ref[...]
    lf_full = jnp.where(real, _log_sigmoid(rs), 0.0)
    lf_ref[0] = lf_full[:, :FOX_HEADS]
    z = _dot(rs.astype(BF16), w2_ref[...]) + ba_ref[...]
    lg_ref[0] = jnp.where(real, _log_sigmoid(z) * (1.0 / GLA_TAU), 0.0)

    carry = carry_ref[...]
    tri = tri_ref[...]
    cs = []
    for sb in range(tm // CUM_BLK):
        hi, mid, lo = _split3(lf_full[sb * CUM_BLK:(sb + 1) * CUM_BLK])
        c = _dot(tri, hi) + _dot(tri, mid) + _dot(tri, lo) + carry
        carry = c[CUM_BLK - 1:CUM_BLK]
        cs.append(c)
    carry_ref[...] = carry
    c = jnp.concatenate(cs, axis=0) * LOG2E

    r = _dot(xb, wqkv_ref[...])
    kf_ref[0] = r[:, FOX_W:2 * FOX_W].T
    vf_ref[0] = r[:, 2 * FOX_W:3 * FOX_W].T

    lane = lax.broadcasted_iota(jnp.int32, (1, LANES), 1)
    hi = c.astype(BF16).astype(F32)
    r1 = c - hi
    mid = r1.astype(BF16).astype(F32)
    lo = r1 - mid
    nh = FOX_HEADS
    parts = jnp.where(lane < nh, hi, jnp.where(lane < 2 * nh, pltpu.roll(mid, nh, axis=1), jnp.where(
        lane < 3 * nh, pltpu.roll(lo, 2 * nh, axis=1), jnp.where(lane == 3 * nh, 1.0, 0.0))))
    ext = _dot(parts.astype(BF16), place_ref[...])
    for h in range(FOX_HEADS):
        p, e = divmod(h, 2)
        dmask = (lane < FOX_DH) if e == 0 else (lane >= FOX_DH)
        xo = FOX_DH if e == 0 else 0
        ev = jnp.where(lane == xo, 1.0, 0.0)
        rq = r[:, p * LANES:(p + 1) * LANES]
        rk = r[:, FOX_W + p * LANES:FOX_W + (p + 1) * LANES]
        rv = r[:, 2 * FOX_W + p * LANES:2 * FOX_W + (p + 1) * LANES]
        q_ref[0, h] = jnp.where(dmask, rq * LOG2E, ext[:, h * LANES:(h + 1) * LANES]).astype(BF16)
        k_ref[0, h] = jnp.where(dmask, rk, ext[:, (nh + h) * LANES:(nh + h + 1) * LANES]).astype(BF16)
        vt = jnp.where(dmask, rv, ev)
        for cb in range(tm // LANES):
            v_ref[0, h, cb] = vt[cb * LANES:(cb + 1) * LANES].T.astype(BF16)

    rg = _dot(xb, wg_ref[...])
    gq_ref[0] = rg[:, :GLA_K]
    gk_ref[0] = rg[:, GLA_K:2 * GLA_K]
    gv_ref[0] = rg[:, 2 * GLA_K:].astype(BF16)


def _const_spec(shape):
    nd = len(shape)
    return pl.BlockSpec(shape, lambda *_: (0,) * nd, pipeline_mode=pl.Buffered(1))


def _layer_spec(w, l):
    nd = w.ndim - 1
    return pl.BlockSpec((None,) + w.shape[1:], lambda *_: (l,) + (0,) * nd,
                        pipeline_mode=pl.Buffered(1))


def _inproj(hp, lw, l, kv_all=None):
    tm = TM_IN
    nt = LP // tm
    tri = jnp.asarray(np.tril(np.ones((CUM_BLK, CUM_BLK), np.float32)), BF16)
    place = jnp.asarray(_fox_place_table(), BF16)
    row3 = lambda w: pl.BlockSpec((1, tm, w), lambda b, i: (b, i, 0))
    head4 = pl.BlockSpec((1, FOX_HEADS, tm, LANES), lambda b, i: (b, 0, i, 0))
    col3 = pl.BlockSpec((None, 1, FOX_W, tm), lambda b, i: (l, b, 0, i))
    vchunks = pl.BlockSpec((1, FOX_HEADS, tm // LANES, LANES, LANES), lambda b, i: (b, 0, i, 0, 0))
    out_shape = (
        jax.ShapeDtypeStruct((BATCH, FOX_HEADS, LP, LANES), BF16),
        jax.ShapeDtypeStruct((BATCH, FOX_HEADS, LP, LANES), BF16),
        jax.ShapeDtypeStruct((BATCH, FOX_HEADS, LP // LANES, LANES, LANES), BF16),
        jax.ShapeDtypeStruct((DEPTH, BATCH, FOX_W, L_REAL), F32),
        jax.ShapeDtypeStruct((DEPTH, BATCH, FOX_W, L_REAL), F32),
        jax.ShapeDtypeStruct((BATCH, LP, FOX_HEADS), F32),
        jax.ShapeDtypeStruct((BATCH, LP, GLA_K), F32),
        jax.ShapeDtypeStruct((BATCH, LP, GLA_K), F32),
        jax.ShapeDtypeStruct((BATCH, LP, GLA_V), BF16),
        jax.ShapeDtypeStruct((BATCH, LP, GLA_K), F32),
    )
    in_specs = ([row3(D_MODEL)]
                + [_layer_spec(lw[n], l) for n in ("wqkv", "wgla", "ws", "w2", "bs", "ba")]
                + [_const_spec(tri.shape), _const_spec(place.shape)])
    args = [hp, lw["wqkv"], lw["wgla"], lw["ws"], lw["w2"], lw["bs"], lw["ba"], tri, place]
    aliases = {}
    if kv_all is not None:
        aliases = {len(args): 3, len(args) + 1: 4}
        in_specs += [pl.BlockSpec(memory_space=pl.ANY)] * 2
        args += list(kv_all)
    return pl.pallas_call(
        functools.partial(_inproj_kernel, tm=tm, n_aliased=len(aliases)),
        grid=(BATCH, nt),
        in_specs=in_specs,
        out_specs=(head4, head4, vchunks, col3, col3, row3(FOX_HEADS),
                   row3(GLA_K), row3(GLA_K), row3(GLA_V), row3(GLA_K)),
        out_shape=out_shape,
        input_output_aliases=aliases,
        scratch_shapes=[pltpu.VMEM((1, LANES), F32)],
        compiler_params=pltpu.CompilerParams(
            dimension_semantics=("parallel", "arbitrary"), vmem_limit_bytes=VMEM_LIMIT),
        name="prompt_inproj",
    )(*args)


def _fox_kernel(q_ref, k_ref, v_ref, o_ref, acc_sc, m_sc, mb_sc, s0_sc, s1_sc, s2_sc):
    i = pl.program_id(2)
    bq, bk = FOX_BQ, FOX_BLK
    nsub = bq // bk
    per_blk = bk // LANES

    for e in range(2):
        m_sc[e] = jnp.full((1, bq), NEG_INF, F32)
        acc_sc[e] = jnp.zeros((LANES, bq), F32)

    def values_t(e, j):
        return jnp.concatenate([v_ref[0, e, per_blk * j + s] for s in range(per_blk)], axis=1)

    sbuf = (s0_sc, s1_sc, s2_sc)
    causal = (lax.broadcasted_iota(jnp.int32, (bk, 1), 0)
              <= lax.broadcasted_iota(jnp.int32, (1, bk), 1))

    def scores(j, slot, c0):
        start = j * bk if isinstance(j, int) else pl.multiple_of(j * bk, bk)
        for e in range(2):
            st = _dot_nt(k_ref[0, e, pl.ds(start, bk), :], q_ref[0, e, c0:, :])
            sbuf[slot][e, :, c0:] = st
            mb_sc[slot, e, :, c0:] = jnp.max(st, axis=0, keepdims=True)

    def step(j, cur, c0, masked, next_c0):
        if next_c0 is not None:
            scores(j + 1, (cur + 1) % nsub, next_c0)
        for e in range(2):
            for sb in range((bq - c0) // bk):
                cols = slice(c0 + sb * bk, c0 + (sb + 1) * bk)
                st = sbuf[cur][e, :, cols]
                if masked and sb == 0:
                    st = jnp.where(causal, st, NEG_INF)
                    mblk = jnp.max(st, axis=0, keepdims=True)
                else:
                    mblk = mb_sc[cur, e, :, cols]
                m_old = m_sc[e, :, cols]
                m_new = jnp.maximum(m_old, mblk)
                alpha = jnp.exp2(m_old - m_new)
                pt = jnp.exp2(st - m_new).astype(BF16)
                acc_sc[e, :, cols] = alpha * acc_sc[e, :, cols] + _dot(values_t(e, j), pt)
                m_sc[e, :, cols] = m_new

    scores(0, 0, 0)

    def body(ii, carry):
        for s in range(nsub):
            step(nsub * ii + s, s, 0, False, 0)
        return carry
    lax.fori_loop(0, i, body, 0)

    for s in range(nsub):
        step(nsub * i + s, s, s * bk, True, (s + 1) * bk if s + 1 < nsub else None)

    a0 = acc_sc[0]
    a1 = acc_sc[1]
    row = lax.broadcasted_iota(jnp.int32, (LANES, 1), 0)
    ot = jnp.where(row < FOX_DH, a0 / a0[FOX_DH:FOX_DH + 1, :], a1 / a1[0:1, :])
    o_ref[0] = ot.T.astype(BF16)


def _fox_prompt(q, k, v):
    nq = LP // FOX_BQ
    score_buf = pltpu.VMEM((2, FOX_BLK, FOX_BQ), F32)
    return pl.pallas_call(
        _fox_kernel,
        grid=(BATCH, FOX_HEADS // 2, nq),
        in_specs=[pl.BlockSpec((1, 2, FOX_BQ, LANES), lambda b, p, i: (b, p, i, 0)),
                  pl.BlockSpec((1, 2, LP, LANES), lambda b, p, i: (b, p, 0, 0)),
                  pl.BlockSpec((1, 2, LP // LANES, LANES, LANES), lambda b, p, i: (b, p, 0, 0, 0))],
        out_specs=pl.BlockSpec((1, FOX_BQ, LANES), lambda b, p, i: (b, i, p)),
        out_shape=jax.ShapeDtypeStruct((BATCH, LP, FOX_W), BF16),
        scratch_shapes=[pltpu.VMEM((2, LANES, FOX_BQ), F32),
                        pltpu.VMEM((2, 1, FOX_BQ), F32),
                        pltpu.VMEM((FOX_BQ // FOX_BLK, 2, 1, FOX_BQ), F32),
                        score_buf, score_buf, score_buf],
        compiler_params=pltpu.CompilerParams(
            dimension_semantics=("parallel", "parallel", "arbitrary"), vmem_limit_bytes=VMEM_LIMIT),
        name="fox_prompt",
    )(q, k, v)


def _gla_tables():
    c = GLA_CHUNK
    t = np.arange(c)[:, None]
    j = np.arange(c)[None, :]
    mats = [(j <= t).astype(np.float32), (j > t).astype(np.float32)]
    masks = [np.eye(c, dtype=np.float32)]
    blk = c
    while blk >= 2:
        half = blk // 2
        mid = (t // blk) * blk + half
        mats.append((j <= t).astype(np.float32) - (j <= mid).astype(np.float32))
        s = j
        masks.append((((t // blk) == (s // blk)) & ((t % blk) >= half) & ((s % blk) < half))
                     .astype(np.float32))
        blk = half
    return np.concatenate(mats, axis=0), np.stack([np.concatenate([m, m], axis=0) for m in masks])


_GLA_LEVELS = 7


def _gla_kernel(q_ref, k_ref, lg_ref, v_ref, dall_ref, masks_ref, o_ref, sfin_ref, st_sc):
    i = pl.program_id(1)
    c = GLA_CHUNK
    npair = GLA_HEADS // 2

    @pl.when(i == 0)
    def _():
        st_sc[...] = jnp.zeros_like(st_sc)

    lane = lax.broadcasted_iota(jnp.int32, (1, LANES), 1)
    hmask = (lane < GLA_DK, lane >= GLA_DK)

    for ci in range(GLA_TILE // c):
        rows = slice(ci * c, (ci + 1) * c)
        for pr in range(npair):
            pl_ = slice(pr * LANES, (pr + 1) * LANES)
            q = q_ref[0, rows, pl_]
            k = k_ref[0, rows, pl_]
            lg = lg_ref[0, rows, pl_]
            hi = lg.astype(BF16)
            lo = (lg - hi.astype(F32)).astype(BF16)
            e2 = _dot(dall_ref[...], jnp.concatenate([hi, lo], axis=1))
            ex = e2[:, :LANES] + e2[:, LANES:]
            bc = ex[0:c]
            q_in = q * jnp.exp(bc)
            k_dec = k * jnp.exp(ex[c:2 * c])
            qs = [q]
            ks = [k.astype(BF16)]
            for lv in range(_GLA_LEVELS):
                f = jnp.exp(-jnp.abs(ex[(2 + lv) * c:(3 + lv) * c]))
                qs.append(q * f)
                ks.append((k * f).astype(BF16))
            decay_all = jnp.exp(bc[c - 1:c])
            a2 = jnp.zeros((2 * c, c), F32)
            for lv in range(_GLA_LEVELS + 1):
                ql = jnp.concatenate([jnp.where(hmask[0], qs[lv], 0.0),
                                      jnp.where(hmask[1], qs[lv], 0.0)], axis=0).astype(BF16)
                a2 = a2 + masks_ref[lv] * _dot_nt(ql, ks[lv])
            for h in range(2):
                hh = 2 * pr + h
                a = a2[h * c:(h + 1) * c]
                vh = v_ref[0, rows, hh * GLA_DV:(hh + 1) * GLA_DV]
                st = st_sc[hh]
                o = _dot(a.astype(BF16), vh) + _dot_nt(
                    jnp.where(hmask[h], q_in, 0.0).astype(BF16), st.astype(BF16))
                o_ref[0, rows, hh * GLA_DV:(hh + 1) * GLA_DV] = o
                kd = jnp.where(hmask[h], k_dec, 0.0).astype(BF16)
                st_sc[hh] = decay_all * st + _dot_tn(vh, kd)

    @pl.when(i == pl.num_programs(1) - 1)
    def _():
        for hh in range(GLA_HEADS):
            s = st_sc[hh].T
            h = hh % 2
            sfin_ref[0, hh] = s[h * GLA_DK:(h + 1) * GLA_DK, :]


def _gla_prompt(gq, gk, lg, gv):
    dall_np, masks_np = _gla_tables()
    dall = jnp.asarray(dall_np, BF16)
    masks = jnp.asarray(masks_np, F32)
    t = GLA_TILE
    qk = pl.BlockSpec((1, t, GLA_K), lambda b, i: (b, i, 0))
    wide = pl.BlockSpec((1, t, GLA_V), lambda b, i: (b, i, 0))
    return pl.pallas_call(
        _gla_kernel,
        grid=(BATCH, LP // t),
        in_specs=[qk, qk, qk, wide, _const_spec(dall.shape), _const_spec(masks.shape)],
        out_specs=(wide, pl.BlockSpec((1, GLA_HEADS, GLA_DK, GLA_DV), lambda b, i: (b, 0, 0, 0))),
        out_shape=(jax.ShapeDtypeStruct((BATCH, LP, GLA_V), F32),
                   jax.ShapeDtypeStruct((BATCH, GLA_HEADS, GLA_DK, GLA_DV), F32)),
        scratch_shapes=[pltpu.VMEM((GLA_HEADS, GLA_DV, LANES), F32)],
        compiler_params=pltpu.CompilerParams(
            dimension_semantics=("parallel", "arbitrary"), vmem_limit_bytes=VMEM_LIMIT),
        name="gla_prompt",
    )(gq, gk, lg, gv, dall, masks)


def _merge_kernel(h_ref, oa_ref, ob_ref, wg_ref, wpa_ref, wpb_ref, wo_ref, gn_ref, g_ref, b_ref,
                  out_ref):
    x = h_ref[...]
    xb = x.astype(BF16)
    r = _dot(xb, wg_ref[...])
    rb = r[:, :GLA_V]
    ga = r[:, GLA_V:GLA_V + D_MODEL]
    gb = r[:, GLA_V + D_MODEL:]
    ob = ob_ref[...]
    parts = []
    for hd in range(GLA_HEADS):
        o = ob[:, hd * GLA_DV:(hd + 1) * GLA_DV]
        ms = jnp.mean(o * o, axis=-1, keepdims=True)
        parts.append(o * lax.rsqrt(ms + NORM_EPS) * gn_ref[...])
    obn = jnp.concatenate(parts, axis=1) * (rb * _sigmoid(rb))
    y_a = _dot(oa_ref[...], wpa_ref[...])
    y_b = _dot(obn.astype(BF16), wpb_ref[...])
    mixed = _sigmoid(ga) * y_a + _sigmoid(gb) * y_b
    y = ALPHA * x + _dot(mixed.astype(BF16), wo_ref[...])
    out_ref[...] = _layer_norm(y, g_ref[...], b_ref[...])


def _merge(h, oa, ob, lw, l, tm):
    m = h.shape[0]
    row = lambda w: pl.BlockSpec((tm, w), lambda i: (i, 0))
    ws = (lw["wmg"], lw["wpa"], lw["wpb"], lw["wo"], lw["gn"], lw["ln1g"], lw["ln1b"])
    return pl.pallas_call(
        _merge_kernel,
        grid=(m // tm,),
        in_specs=[row(D_MODEL), row(FOX_W), row(GLA_V)] + [_layer_spec(w, l) for w in ws],
        out_specs=row(D_MODEL),
        out_shape=jax.ShapeDtypeStruct((m, D_MODEL), F32),
        compiler_params=pltpu.CompilerParams(
            dimension_semantics=("parallel",), vmem_limit_bytes=VMEM_LIMIT),
        name="merge",
    )(h, oa, ob, *ws)


def _ffn_kernel(h_ref, wgate_ref, wup_ref, wdown_ref, g_ref, b_ref, out_ref):
    x = h_ref[...]
    xb = x.astype(BF16)
    gt = _dot(xb, wgate_ref[...])
    up = _dot(xb, wup_ref[...])
    hdn = (gt * _sigmoid(gt) * up).astype(BF16)
    y = ALPHA * x + _dot(hdn, wdown_ref[...])
    out_ref[...] = _layer_norm(y, g_ref[...], b_ref[...])


def _ffn(h, lw, l, tm):
    m = h.shape[0]
    row = pl.BlockSpec((tm, D_MODEL), lambda i: (i, 0))
    ws = (lw["wgate"], lw["wup"], lw["wdown"], lw["ln2g"], lw["ln2b"])
    return pl.pallas_call(
        _ffn_kernel,
        grid=(m // tm,),
        in_specs=[row] + [_layer_spec(w, l) for w in ws],
        out_specs=row,
        out_shape=jax.ShapeDtypeStruct((m, D_MODEL), F32),
        compiler_params=pltpu.CompilerParams(
            dimension_semantics=("parallel",), vmem_limit_bytes=VMEM_LIMIT),
        name="ffn",
    )(h, *ws)


def _sample_inproj_kernel(x_ref, wqkv_ref, wg_ref, ws_ref, w2_ref, bs_ref, ba_ref,
                          r1_ref, r2_ref, lf_ref, lg_ref):
    xb = x_ref[...].astype(BF16)
    r1_ref[...] = _dot(xb, wqkv_ref[...])
    r2_ref[...] = _dot(xb, wg_ref[...])
    rs = _dot(xb, ws_ref[...]) + bs_ref[...]
    lf_ref[...] = _log_sigmoid(rs)
    z = _dot(rs.astype(BF16), w2_ref[...]) + ba_ref[...]
    lg_ref[...] = _log_sigmoid(z) * (1.0 / GLA_TAU)


def _sample_inproj(x, lw, l):
    ws = (lw["wqkv"], lw["wgla"], lw["ws"], lw["w2"], lw["bs"], lw["ba"])
    full = lambda shape: pl.BlockSpec(shape, lambda i: (0,) * len(shape))
    out_shape = (jax.ShapeDtypeStruct((DEC_BATCH, 3 * FOX_W), F32),
                 jax.ShapeDtypeStruct((DEC_BATCH, 2 * GLA_K + GLA_V), F32),
                 jax.ShapeDtypeStruct((DEC_BATCH, LANES), F32),
                 jax.ShapeDtypeStruct((DEC_BATCH, GLA_K), F32))
    return pl.pallas_call(
        _sample_inproj_kernel,
        grid=(1,),
        in_specs=[full(x.shape)] + [_layer_spec(w, l) for w in ws],
        out_specs=tuple(full(s.shape) for s in out_shape),
        out_shape=out_shape,
        compiler_params=pltpu.CompilerParams(vmem_limit_bytes=VMEM_LIMIT),
        name="sample_inproj",
    )(x, *ws)


def _decode_tables():
    j = np.arange(LANES)
    ut = np.concatenate([j[:, None] > j[None, :], np.ones((LANES, LANES), bool)], axis=1)
    nr = PAGES_PER_STEP * FOX_HEADS
    r = np.arange(nr)
    same = (r[:, None] % FOX_HEADS) == (r[None, :] % FOX_HEADS)
    us = np.zeros((nr + 2 * SUBLANES, nr), bool)
    us[:nr] = same & (r[None, :] > r[:, None])
    us[nr:nr + FOX_HEADS] = (r[None, :] % FOX_HEADS) == np.arange(FOX_HEADS)[:, None]
    return ut.astype(np.float32), us.astype(np.float32)


def _fox_decode_kernel(pt_ref, qbd_ref, knb_ref, vrow_ref, lfn_ref, ut_ref, us_ref, *rest):
    npg = PAGES_PER_STEP
    kp = rest[0:npg]
    vp = rest[npg:2 * npg]
    lft_ref = rest[2 * npg]
    o_ref = rest[2 * npg + 1]
    m_sc, l_sc, acc_sc, carry_sc = rest[2 * npg + 2:]
    b = pl.program_id(0)
    j = pl.program_id(1)
    first_page = (pl.num_programs(1) - 1 - j) * npg
    nh = FOX_HEADS
    nr = npg * nh
    qbd = qbd_ref[0]

    def page2d(ref):
        return ref[...].reshape(FOX_W, PAGE_SIZE).astype(BF16)

    @pl.when(j == 0)
    def _():
        m_sc[...] = _dot(qbd, page2d(knb_ref.at[0]))
        lane = lax.broadcasted_iota(jnp.int32, (nh, LANES), 1)
        l_sc[...] = jnp.where(lane == 0, 1.0, 0.0)
        acc_sc[...] = jnp.broadcast_to(vrow_ref[0].astype(BF16).astype(F32), (nh, FOX_W))
        carry_sc[...] = lfn_ref[0]

    lfc = jnp.concatenate([lft_ref[pt_ref[b, first_page + g]] for g in range(npg)],
                          axis=0)
    w = _dot(jnp.concatenate(_split3(lfc), axis=0), ut_ref[...])

    s = [_dot(qbd, page2d(kp[g])) for g in range(npg)]

    wsum = w[0:nr] + w[nr:2 * nr] + w[2 * nr:3 * nr]
    x = _dot(us_ref[...], jnp.concatenate(_split3(wsum[:, LANES:]), axis=1))
    xs = x[:, :LANES] + x[:, LANES:2 * LANES] + x[:, 2 * LANES:]
    carry = carry_sc[...]
    sb = [s[g] + (wsum[g * nh:(g + 1) * nh, :LANES] + xs[g * nh:(g + 1) * nh] + carry)
          for g in range(npg)]
    carry_sc[...] = carry + xs[nr:nr + nh]

    mx = sb[0]
    for g in range(1, npg):
        mx = jnp.maximum(mx, sb[g])
    m_old = m_sc[...]
    m_new = jnp.maximum(m_old, jnp.max(mx, axis=1, keepdims=True))
    alpha = jnp.exp(m_old - m_new)
    p = [jnp.exp(sb[g] - m_new) for g in range(npg)]
    psum = p[0]
    for g in range(1, npg):
        psum = psum + p[g]
    l_sc[...] = alpha * l_sc[...] + psum
    m_sc[...] = m_new
    pv = _dot_nt(p[0].astype(BF16), page2d(vp[0]))
    for g in range(1, npg):
        pv = pv + _dot_nt(p[g].astype(BF16), page2d(vp[g]))
    acc_sc[...] = jnp.concatenate([alpha] * (FOX_W // LANES), axis=1) * acc_sc[...] + pv

    @pl.when(j == pl.num_programs(1) - 1)
    def _():
        ltot = jnp.sum(l_sc[...], axis=1, keepdims=True)
        own = (lax.broadcasted_iota(jnp.int32, (nh, FOX_W), 1) // FOX_DH
               == lax.broadcasted_iota(jnp.int32, (nh, FOX_W), 0))
        o_ref[0] = jnp.sum(jnp.where(own, acc_sc[...] / ltot, 0.0), axis=0, keepdims=True)


def _fox_decode(layer, page_table, qbd, knb, vrow, lfn, cache_kt, cache_vt, cache_lft):
    npg = PAGES_PER_STEP
    nsteps = N_PAGES // npg
    ut_np, us_np = _decode_tables()
    ut = jnp.asarray(ut_np, BF16)
    us = jnp.asarray(us_np, BF16)

    def page_map(g, tail):
        def f(b, j, pt):
            return (layer, pt[b, (nsteps - 1 - j) * npg + g]) + tail
        return f

    kv_specs = [pl.BlockSpec((None, None, FOX_HEADS, FOX_DH, PAGE_SIZE), page_map(g, (0, 0, 0)))
                for g in range(npg)]
    lf_spec = pl.BlockSpec((None,) + cache_lft.shape[1:], lambda b, j, pt: (layer, 0, 0, 0),
                           pipeline_mode=pl.Buffered(1))
    per_b = lambda shape: pl.BlockSpec((1,) + shape, lambda b, j, pt: (b,) + (0,) * len(shape))
    const = lambda shape: pl.BlockSpec(shape, lambda b, j, pt: (0,) * len(shape))
    hdl = (FOX_HEADS, FOX_DH, LANES)
    grid_spec = pltpu.PrefetchScalarGridSpec(
        num_scalar_prefetch=1,
        grid=(DEC_BATCH, nsteps),
        in_specs=[per_b((FOX_HEADS, FOX_W)), per_b(hdl), per_b((1, FOX_W)),
                  per_b((FOX_HEADS, LANES)), const(ut.shape), const(us.shape)]
        + kv_specs + kv_specs + [lf_spec],
        out_specs=per_b((1, FOX_W)),
        scratch_shapes=[pltpu.VMEM((FOX_HEADS, LANES), F32), pltpu.VMEM((FOX_HEADS, LANES), F32),
                        pltpu.VMEM((FOX_HEADS, FOX_W), F32), pltpu.VMEM((FOX_HEADS, LANES), F32)],
    )
    return pl.pallas_call(
        _fox_decode_kernel,
        grid_spec=grid_spec,
        out_shape=jax.ShapeDtypeStruct((DEC_BATCH, 1, FOX_W), F32),
        compiler_params=pltpu.CompilerParams(
            dimension_semantics=("parallel", "arbitrary"), vmem_limit_bytes=VMEM_LIMIT),
        name="fox_decode",
    )(page_table, qbd, knb, vrow, lfn, ut, us,
      *([cache_kt] * npg), *([cache_vt] * npg), cache_lft)


def _gla_decode_kernel(q_ref, k_ref, g_ref, v_ref, s_ref, o_ref, sn_ref):
    eye = (lax.broadcasted_iota(jnp.int32, (GLA_DK, GLA_DK), 0)
           == lax.broadcasted_iota(jnp.int32, (GLA_DK, GLA_DK), 1))

    def col(r):
        return jnp.sum(jnp.where(eye, jnp.broadcast_to(r, (GLA_DK, GLA_DK)), 0.0),
                       axis=1, keepdims=True)

    for r in range(GLA_DEC_ROWS):
        for h in range(GLA_HEADS):
            qc = col(q_ref[r, h:h + 1, :])
            kc = col(k_ref[r, h:h + 1, :])
            ac = col(jnp.exp(g_ref[r, h:h + 1, :]))
            sn = ac * s_ref[r, h] + kc * v_ref[r, h:h + 1, :]
            sn_ref[r, h] = sn
            o_ref[r, h:h + 1, :] = jnp.sum(qc * sn, axis=0, keepdims=True)


def _gla_decode(gq, gk, lg, gv, state):
    nb = GLA_DEC_ROWS
    hk = pl.BlockSpec((nb, GLA_HEADS, GLA_DK), lambda b: (b, 0, 0))
    hv = pl.BlockSpec((nb, GLA_HEADS, GLA_DV), lambda b: (b, 0, 0))
    st = pl.BlockSpec((nb, GLA_HEADS, GLA_DK, GLA_DV), lambda b: (b, 0, 0, 0))
    return pl.pallas_call(
        _gla_decode_kernel,
        grid=(DEC_BATCH // nb,),
        in_specs=[hk, hk, hk, hv, st],
        out_specs=(hv, st),
        out_shape=(jax.ShapeDtypeStruct((DEC_BATCH, GLA_HEADS, GLA_DV), F32),
                   jax.ShapeDtypeStruct((DEC_BATCH, GLA_HEADS, GLA_DK, GLA_DV), F32)),
        compiler_params=pltpu.CompilerParams(
            dimension_semantics=("parallel",), vmem_limit_bytes=VMEM_LIMIT),
        name="gla_decode",
    )(gq, gk, lg, gv, state)


def _stacked_weights(w_in, b_f, w_a2, b_a, gla_norm_g, w_pa, w_pb, w_o, ln1_g, ln1_b,
                     w_gate, w_up, w_down, ln2_g, ln2_b):
    o = _OFF
    sc_f = FOX_DH ** -0.5
    sc_g = GLA_DK ** -0.5
    seg = lambda a: w_in[:, :, o[a]:o[a + 1]]
    wqkv = jnp.concatenate([seg(0) * sc_f, seg(1), seg(2)], axis=2).astype(BF16)
    wgla = jnp.concatenate([seg(4) * sc_g, seg(5), seg(6)], axis=2).astype(BF16)
    ws = jnp.zeros((DEPTH, D_MODEL, LANES), F32)
    ws = ws.at[:, :, :FOX_HEADS].set(seg(3)).at[:, :, FOX_HEADS:FOX_HEADS + GLA_RANK].set(seg(8))
    w2 = jnp.zeros((DEPTH, LANES, GLA_K), F32).at[:, FOX_HEADS:FOX_HEADS + GLA_RANK].set(w_a2)
    bs = jnp.zeros((DEPTH, 1, LANES), F32).at[:, 0, :FOX_HEADS].set(b_f)
    wmg = jnp.concatenate([seg(7), seg(9), seg(10)], axis=2).astype(BF16)
    vec = lambda t: t[:, None, :]
    return dict(
        wqkv=wqkv, wgla=wgla, ws=ws.astype(BF16), w2=w2.astype(BF16), bs=bs, ba=vec(b_a),
        wmg=wmg, wpa=w_pa.astype(BF16), wpb=w_pb.astype(BF16), wo=w_o.astype(BF16),
        gn=vec(gla_norm_g), ln1g=vec(ln1_g), ln1b=vec(ln1_b),
        wgate=w_gate.astype(BF16), wup=w_up.astype(BF16), wdown=w_down.astype(BF16),
        ln2g=vec(ln2_g), ln2b=vec(ln2_b))


def kernel(x_prompt, x_sample, cache_k, cache_v, cache_lf, state_gla, page_table, meta, w_in, b_f,
           w_a2, b_a, gla_norm_g, w_pa, w_pb, w_o, ln1_g, ln1_b, w_gate, w_up, w_down, ln2_g, ln2_b):
    assert x_prompt.shape == (BATCH, SEQ, D_MODEL) and x_sample.shape == (DEC_BATCH, 1, D_MODEL)
    hp = jnp.concatenate([jnp.broadcast_to(meta.astype(F32)[None], (BATCH, N_META, D_MODEL)), x_prompt,
                          jnp.zeros((BATCH, PAD_ROWS, D_MODEL), F32)], axis=1)
    hs = x_sample.reshape(DEC_BATCH, D_MODEL)
    cache_kt = jnp.transpose(cache_k, (0, 1, 3, 4, 2))
    cache_vt = jnp.transpose(cache_v, (0, 1, 3, 4, 2))
    cache_lft = jnp.transpose(cache_lf, (0, 1, 3, 2))

    lw = _stacked_weights(w_in, b_f, w_a2, b_a, gla_norm_g, w_pa, w_pb, w_o, ln1_g, ln1_b,
                          w_gate, w_up, w_down, ln2_g, ln2_b)
    kp, vp, lfp, gp, ksr, vsr, lfs, gs = [], [], [], [], [], [], [], []
    for l in range(DEPTH):
        qa, ka, va, kf, vf, lf, gq, gk, gv, lg = _inproj(hp, lw, l, None if l == 0 else (kf, vf))
        oa = _fox_prompt(qa, ka, va)
        ob, sfin = _gla_prompt(gq, gk, lg, gv)
        h2 = hp.reshape(BATCH * LP, D_MODEL)
        h2 = _merge(h2, oa.reshape(BATCH * LP, FOX_W), ob.reshape(BATCH * LP, GLA_V), lw, l, TM_TOK)
        h2 = _ffn(h2, lw, l, TM_TOK)
        hp = h2.reshape(BATCH, LP, D_MODEL)
        lfp.append(lf[:, :L_REAL])
        gp.append(sfin)
        r1, r2, lfs_full, lgs = _sample_inproj(hs, lw, l)
        q_s = r1[:, :FOX_W].reshape(DEC_BATCH, FOX_HEADS, FOX_DH)
        k_s = r1[:, FOX_W:2 * FOX_W].reshape(DEC_BATCH, FOX_HEADS, FOX_DH)
        v_s = r1[:, 2 * FOX_W:].reshape(DEC_BATCH, FOX_HEADS, FOX_DH)
        lf_s = lfs_full[:, :FOX_HEADS]
        lanes = lambda t: jnp.broadcast_to(t[..., None], t.shape + (LANES,))
        eye = jnp.eye(FOX_HEADS, dtype=F32)
        qbd = (q_s[:, :, None, :] * eye[None, :, :, None]).reshape(DEC_BATCH, FOX_HEADS, FOX_W)
        oa_s = _fox_decode(l, page_table, qbd.astype(BF16), lanes(k_s), r1[:, None, 2 * FOX_W:],
                           lanes(lf_s), cache_kt, cache_vt, cache_lft)
        ob_s, s_new = _gla_decode(r2[:, :GLA_K].reshape(DEC_BATCH, GLA_HEADS, GLA_DK),
                                  r2[:, GLA_K:2 * GLA_K].reshape(DEC_BATCH, GLA_HEADS, GLA_DK),
                                  lgs.reshape(DEC_BATCH, GLA_HEADS, GLA_DK),
                                  r2[:, 2 * GLA_K:].reshape(DEC_BATCH, GLA_HEADS, GLA_DV),
                                  state_gla[l])
        hs = _merge(hs, oa_s.reshape(DEC_BATCH, FOX_W).astype(BF16),
                    ob_s.reshape(DEC_BATCH, GLA_V), lw, l, DEC_BATCH)
        hs = _ffn(hs, lw, l, DEC_BATCH)
        ksr.append(k_s[:, None])
        vsr.append(v_s[:, None])
        lfs.append(lf_s[:, None])
        gs.append(s_new)

    y_prompt = hp[:, N_META:L_REAL]
    y_sample = hs[:, None, :]
    rows = lambda t: jnp.transpose(
        t.reshape(DEPTH, BATCH, FOX_HEADS, FOX_DH, L_REAL), (0, 1, 4, 2, 3))
    return (y_prompt, y_sample, rows(kf), rows(vf), jnp.stack(lfp), jnp.stack(gp),
            jnp.stack(ksr), jnp.stack(vsr), jnp.stack(lfs), jnp.stack(gs))
```
